```python
import jax, jax.numpy as jnp
from jax import lax
import numpy as np

D_MODEL = 1024
BATCH = 2
SEQ = 8192
DEPTH = 4
DEC_BATCH = 16
DEC_SEQ = 2048
PAST_LEN = 128

GRID_W = 64
ROPE_THETA = 10000.0
NORM_EPS = 1e-6

ATTN_HEADS = 8
ATTN_KV_HEADS = 2
ATTN_GROUP = ATTN_HEADS // ATTN_KV_HEADS
ATTN_HEAD_DIM = 64
ATTN_Q_WIDTH = ATTN_HEADS * ATTN_HEAD_DIM
ATTN_KV_WIDTH = ATTN_KV_HEADS * ATTN_HEAD_DIM
QUERY_BLOCK = 128

MLSTM_HEADS = 4
MLSTM_HEAD_DIM = 128
MLSTM_WIDTH = MLSTM_HEADS * MLSTM_HEAD_DIM
MLSTM_CHUNK = 128
MLSTM_GATES = 4 * MLSTM_HEADS

AB_SIZES = (ATTN_Q_WIDTH, ATTN_KV_WIDTH, ATTN_KV_WIDTH,
            MLSTM_WIDTH, MLSTM_WIDTH, MLSTM_WIDTH, MLSTM_WIDTH, MLSTM_GATES)
AB_IN_COLS = sum(AB_SIZES)
AB_SPLIT_POINTS = tuple(int(p) for p in np.cumsum(AB_SIZES)[:-1])
AB_MIX_WIDTH = ATTN_Q_WIDTH + MLSTM_WIDTH

RET_HEADS = 4
RET_QK_DIM = 256
RET_V_DIM = 512
RET_QK_WIDTH = RET_HEADS * RET_QK_DIM
RET_V_WIDTH = RET_HEADS * RET_V_DIM
RET_CHUNK = 128
RET_SIZES = (RET_QK_WIDTH, RET_QK_WIDTH, RET_V_WIDTH, RET_V_WIDTH)
RET_IN_COLS = sum(RET_SIZES)
RET_SPLIT_POINTS = tuple(int(p) for p in np.cumsum(RET_SIZES)[:-1])

D_FF = 2816
CONV_WIDTH = 3

N_AB_LAYERS = (DEPTH + 1) // 2
N_RET_LAYERS = DEPTH // 2

kernel_name = 'hybrid_bidir_attn_mlstm_retention_convffn'


def rms_norm(x, gain):
    xf = x.astype(jnp.float32)
    y = xf * lax.rsqrt(jnp.mean(xf * xf, axis=-1, keepdims=True) + NORM_EPS)
    return (y * gain.astype(jnp.float32)).astype(x.dtype)


def head_rms_norm(x, gain):
    xf = x.astype(jnp.float32)
    return xf * lax.rsqrt(jnp.mean(xf * xf, axis=-1, keepdims=True) + NORM_EPS) * gain.astype(jnp.float32)


def flip_seq(a):
    return jnp.flip(a, axis=1)


def axial_rope_tables(seq_len, head_dim):
    rows = seq_len // GRID_W
    row_idx = jnp.repeat(jnp.arange(rows, dtype=jnp.float32), GRID_W)
    col_idx = jnp.tile(jnp.arange(GRID_W, dtype=jnp.float32), rows)
    axis_dim = head_dim // 2
    inv_freq = ROPE_THETA ** (-jnp.arange(0, axis_dim, 2, dtype=jnp.float32) / axis_dim)
    ang = jnp.concatenate([row_idx[:, None] * inv_freq, col_idx[:, None] * inv_freq], axis=-1)
    return jnp.cos(ang), jnp.sin(ang)


def apply_rope(x, cos, sin):
    half = x.shape[-1] // 2
    x1, x2 = x[..., :half], x[..., half:]
    c = cos[None, :, None, :]
    s = sin[None, :, None, :]
    return jnp.concatenate([x1 * c - x2 * s, x2 * c + x1 * s], axis=-1)


def bidirectional_gqa(q, k, v):
    B, S, _, d = q.shape
    nblk = S // QUERY_BLOCK
    qb = q.reshape(B, nblk, QUERY_BLOCK, ATTN_KV_HEADS, ATTN_GROUP, d).transpose(1, 0, 2, 3, 4, 5) * (d ** -0.5)

    def block(q_blk):
        s = jnp.einsum('bqhgd,bkhd->bhgqk', q_blk, k)
        p = jax.nn.softmax(s, axis=-1)
        return jnp.einsum('bhgqk,bkhd->bqhgd', p, v)

    o = lax.map(block, qb)
    return o.transpose(1, 0, 2, 3, 4, 5).reshape(B, S, ATTN_Q_WIDTH)


def mlstm_causal(q, k, v, log_i, log_f):
    B, S, H, dk = q.shape
    dv = v.shape[-1]
    L = MLSTM_CHUNK
    n_chunks = S // L

    def chunks(a):
        return jnp.moveaxis(a.reshape((B, n_chunks, L) + a.shape[2:]), 1, 0)

    mask = jnp.tril(jnp.ones((L, L), dtype=bool))

    def step(carry, xs):
        C, nv, m = carry
        qc, kc, vc, ic, fc = xs
        b = jnp.cumsum(fc, axis=1).transpose(0, 2, 1)
        ig = ic.transpose(0, 2, 1)
        dlog = jnp.where(mask, b[..., :, None] - b[..., None, :] + ig[..., None, :], -jnp.inf)
        inter = b + m[..., None]
        m_row = jnp.maximum(inter, jnp.max(dlog, axis=-1))
        w = jnp.exp(dlog - m_row[..., None]) * jnp.einsum('blhd,bkhd->bhlk', qc, kc)
        inter_w = jnp.exp(inter - m_row)
        num = (jnp.einsum('bhlk,bkhe->blhe', w, vc)
               + inter_w.transpose(0, 2, 1)[..., None] * jnp.einsum('blhd,bhde->blhe', qc, C))
        den = jnp.sum(w, axis=-1) + inter_w * jnp.einsum('blhd,bhd->bhl', qc, nv)
        h = num / jnp.maximum(jnp.abs(den), jnp.exp(-m_row)).transpose(0, 2, 1)[..., None]
        b_last = b[..., -1]
        wlog = b_last[..., None] - b + ig
        m_new = jnp.maximum(b_last + m, jnp.max(wlog, axis=-1))
        keep = jnp.exp(b_last + m - m_new)
        wk = jnp.exp(wlog - m_new[..., None])
        C_new = keep[..., None, None] * C + jnp.einsum('bhl,blhd,blhe->bhde', wk, kc, vc)
        n_new = keep[..., None] * nv + jnp.einsum('bhl,blhd->bhd', wk, kc)
        return (C_new, n_new, m_new), h

    init = (jnp.zeros((B, H, dk, dv), jnp.float32),
            jnp.zeros((B, H, dk), jnp.float32),
            jnp.zeros((B, H), jnp.float32))
    _, hs = lax.scan(step, init, (chunks(q), chunks(k), chunks(v), chunks(log_i), chunks(log_f)))
    return jnp.moveaxis(hs, 0, 1).reshape(B, S, H, dv)


def retention_causal(q, k, v, log_gamma, strict):
    B, S, H, dk = q.shape
    dv = v.shape[-1]
    L = RET_CHUNK
    n_chunks = S // L
    pos = jnp.arange(L, dtype=jnp.float32)
    diff = pos[:, None] - pos[None, :]
    mask = (diff > 0) if strict else (diff >= 0)
    intra_decay = jnp.where(mask[None], jnp.exp(jnp.maximum(diff, 0.0)[None] * log_gamma[:, None, None]), 0.0)
    query_decay = jnp.exp((pos + 1.0)[None, :] * log_gamma[:, None]).T
    key_decay = jnp.exp((L - 1.0 - pos)[None, :] * log_gamma[:, None])
    chunk_decay = jnp.exp(L * log_gamma)

    def chunks(a):
        return jnp.moveaxis(a.reshape((B, n_chunks, L) + a.shape[2:]), 1, 0)

    def step(state, xs):
        qc, kc, vc = xs
        scores = jnp.einsum('blhd,bkhd->bhlk', qc, kc) * intra_decay
        out = (jnp.einsum('bhlk,bkhe->blhe', scores, vc)
               + jnp.einsum('blhd,bhde->blhe', qc, state) * query_decay[None, :, :, None])
        state_new = chunk_decay[None, :, None, None] * state + jnp.einsum('hl,blhd,blhe->bhde', key_decay, kc, vc)
        return state_new, out

    init = jnp.zeros((B, H, dk, dv), jnp.float32)
    _, outs = lax.scan(step, init, (chunks(q), chunks(k), chunks(v)))
    return jnp.moveaxis(outs, 0, 1).reshape(B, S, H, dv)


def attn_mlstm_mixer(h, w_in, gate_bias, q_norm, k_norm, out_norm, w_out, cos_a, sin_a):
    B, S, _ = h.shape
    f32 = jnp.float32
    aq, ak, av, mq, mk, mv, mo, gates = jnp.split(h @ w_in, AB_SPLIT_POINTS, axis=-1)
    aq = apply_rope(head_rms_norm(aq.reshape(B, S, ATTN_HEADS, ATTN_HEAD_DIM), q_norm), cos_a, sin_a)
    ak = apply_rope(head_rms_norm(ak.reshape(B, S, ATTN_KV_HEADS, ATTN_HEAD_DIM), k_norm), cos_a, sin_a)
    av = av.reshape(B, S, ATTN_KV_HEADS, ATTN_HEAD_DIM).astype(f32)
    attn_out = bidirectional_gqa(aq, ak, av)
    mq = mq.reshape(B, S, MLSTM_HEADS, MLSTM_HEAD_DIM).astype(f32)
    mk = mk.reshape(B, S, MLSTM_HEADS, MLSTM_HEAD_DIM).astype(f32) * (MLSTM_HEAD_DIM ** -0.5)
    mv = mv.reshape(B, S, MLSTM_HEADS, MLSTM_HEAD_DIM).astype(f32)
    g = (gates.astype(f32) + gate_bias.astype(f32)).reshape(B, S, 4, MLSTM_HEADS)
    i_fwd, f_fwd = g[:, :, 0], jax.nn.log_sigmoid(g[:, :, 1])
    i_bwd, f_bwd = g[:, :, 2], jax.nn.log_sigmoid(g[:, :, 3])
    h_fwd = mlstm_causal(mq, mk, mv, i_fwd, f_fwd)
    h_bwd = flip_seq(mlstm_causal(flip_seq(mq), flip_seq(mk), flip_seq(mv), flip_seq(i_bwd), flip_seq(f_bwd)))
    m_out = head_rms_norm(h_fwd + h_bwd, out_norm.reshape(MLSTM_HEADS, MLSTM_HEAD_DIM)).reshape(B, S, MLSTM_WIDTH)
    m_out = m_out * jax.nn.sigmoid(mo.astype(f32))
    mixed = jnp.concatenate([attn_out, m_out], axis=-1).astype(h.dtype)
    return mixed @ w_out


def retention_mixer(h, w_in, decay_logit, out_norm, w_out, cos_r, sin_r):
    B, S, _ = h.shape
    f32 = jnp.float32
    rq, rk, rv, rg = jnp.split(h @ w_in, RET_SPLIT_POINTS, axis=-1)
    rq = apply_rope(rq.reshape(B, S, RET_HEADS, RET_QK_DIM).astype(f32), cos_r, sin_r)
    rk = apply_rope(rk.reshape(B, S, RET_HEADS, RET_QK_DIM).astype(f32), cos_r, sin_r) * (RET_QK_DIM ** -0.5)
    rv = rv.reshape(B, S, RET_HEADS, RET_V_DIM).astype(f32)
    log_gamma = jax.nn.log_sigmoid(decay_logit.astype(f32))
    y = (retention_causal(rq, rk, rv, log_gamma[0], False)
         + flip_seq(retention_causal(flip_seq(rq), flip_seq(rk), flip_seq(rv), log_gamma[1], True)))
    y = head_rms_norm(y, out_norm.reshape(RET_HEADS, RET_V_DIM)).reshape(B, S, RET_V_WIDTH)
    y = y * jax.nn.silu(rg.astype(f32))
    return y.astype(h.dtype) @ w_out


def conv_ffn(h, w_up, conv_w, conv_b, w_down):
    u, g = jnp.split(h @ w_up, 2, axis=-1)
    gp = jnp.pad(g, ((0, 0), (1, 1), (0, 0)))
    g = gp[:, :-2] * conv_w[0] + gp[:, 1:-1] * conv_w[1] + gp[:, 2:] * conv_w[2] + conv_b
    return (jax.nn.gelu(g, approximate=False) * u) @ w_down


def trunk(x, norm_mix, norm_ffn, norm_final, ab_w_in, ab_gate_bias, attn_q_norm, attn_k_norm,
          mlstm_out_norm, ab_w_out, ret_w_in, ret_decay_logit, ret_out_norm, ret_w_out,
          ffn_w_up, ffn_conv_w, ffn_conv_b, ffn_w_down):
    S = x.shape[1]
    cos_a, sin_a = axial_rope_tables(S, ATTN_HEAD_DIM)
    cos_r, sin_r = axial_rope_tables(S, RET_QK_DIM)
    for layer in range(DEPTH):
        j = layer // 2
        h = rms_norm(x, norm_mix[layer])
        if layer % 2 == 0:
            x = x + attn_mlstm_mixer(h, ab_w_in[j], ab_gate_bias[j], attn_q_norm[j], attn_k_norm[j],
                                     mlstm_out_norm[j], ab_w_out[j], cos_a, sin_a)
        else:
            x = x + retention_mixer(h, ret_w_in[j], ret_decay_logit[j], ret_out_norm[j], ret_w_out[j],
                                    cos_r, sin_r)
        x = x + conv_ffn(rms_norm(x, norm_ffn[layer]), ffn_w_up[layer], ffn_conv_w[layer],
                         ffn_conv_b[layer], ffn_w_down[layer])
    return rms_norm(x, norm_final)


def setup_inputs(seed: int = 0) -> dict:
    key = jax.random.key(seed)
    ks = jax.random.split(key, 20)
    nrm = jax.random.normal
    out_scale = (2.0 * DEPTH) ** -0.5
    f_bias = jnp.linspace(3.0, 6.0, MLSTM_HEADS, dtype=jnp.float32)
    zeros_h = jnp.zeros((MLSTM_HEADS,), jnp.float32)
    gate_base = jnp.concatenate([zeros_h, f_bias, zeros_h, f_bias])
    a = 5.0 + jnp.arange(RET_HEADS, dtype=jnp.float32)
    decay_base = jnp.log(2.0 ** a - 1.0)
    return {
        'x_prompt': nrm(ks[0], (BATCH, SEQ, D_MODEL), jnp.float32),
        'x_sample': nrm(ks[1], (DEC_BATCH, DEC_SEQ, D_MODEL), jnp.float32),
        'norm_mix': 1.0 + 0.05 * nrm(ks[2], (DEPTH, D_MODEL), jnp.float32),
        'norm_ffn': 1.0 + 0.05 * nrm(ks[3], (DEPTH, D_MODEL), jnp.float32),
        'norm_final': 1.0 + 0.05 * nrm(ks[4], (D_MODEL,), jnp.float32),
        'ab_w_in': nrm(ks[5], (N_AB_LAYERS, D_MODEL, AB_IN_COLS), jnp.float32) * D_MODEL ** -0.5,
        'ab_gate_bias': gate_base + 0.1 * nrm(ks[6], (N_AB_LAYERS, MLSTM_GATES), jnp.float32),
        'attn_q_norm': 1.0 + 0.05 * nrm(ks[7], (N_AB_LAYERS, ATTN_HEAD_DIM), jnp.float32),
        'attn_k_norm': 1.0 + 0.05 * nrm(ks[8], (N_AB_LAYERS, ATTN_HEAD_DIM), jnp.float32),
        'mlstm_out_norm': 1.0 + 0.05 * nrm(ks[9], (N_AB_LAYERS, MLSTM_WIDTH), jnp.float32),
        'ab_w_out': nrm(ks[10], (N_AB_LAYERS, AB_MIX_WIDTH, D_MODEL), jnp.float32) * AB_MIX_WIDTH ** -0.5 * out_scale,
        'ret_w_in': nrm(ks[11], (N_RET_LAYERS, D_MODEL, RET_IN_COLS), jnp.float32) * D_MODEL ** -0.5,
        'ret_decay_logit': decay_base + 0.05 * nrm(ks[12], (N_RET_LAYERS, 2, RET_HEADS), jnp.float32),
        'ret_out_norm': 1.0 + 0.05 * nrm(ks[13], (N_RET_LAYERS, RET_V_WIDTH), jnp.float32),
        'ret_w_out': nrm(ks[14], (N_RET_LAYERS, RET_V_WIDTH, D_MODEL), jnp.float32) * RET_V_WIDTH ** -0.5 * out_scale,
        'ffn_w_up': nrm(ks[15], (DEPTH, D_MODEL, 2 * D_FF), jnp.float32) * D_MODEL ** -0.5,
        'ffn_conv_w': nrm(ks[16], (DEPTH, CONV_WIDTH, D_FF), jnp.float32) * CONV_WIDTH ** -0.5,
        'ffn_conv_b': 0.02 * nrm(ks[17], (DEPTH, D_FF), jnp.float32),
        'ffn_w_down': nrm(ks[18], (DEPTH, D_FF, D_MODEL), jnp.float32) * D_FF ** -0.5 * out_scale,
    }


def reference(x_prompt, x_sample, norm_mix, norm_ffn, norm_final, ab_w_in, ab_gate_bias, attn_q_norm,
              attn_k_norm, mlstm_out_norm, ab_w_out, ret_w_in, ret_decay_logit, ret_out_norm, ret_w_out,
              ffn_w_up, ffn_conv_w, ffn_conv_b, ffn_w_down):
    y_prompt = trunk(x_prompt, norm_mix, norm_ffn, norm_final, ab_w_in, ab_gate_bias, attn_q_norm,
                     attn_k_norm, mlstm_out_norm, ab_w_out, ret_w_in, ret_decay_logit, ret_out_norm,
                     ret_w_out, ffn_w_up, ffn_conv_w, ffn_conv_b, ffn_w_down)
    y_sample = trunk(x_sample, norm_mix, norm_ffn, norm_final, ab_w_in, ab_gate_bias, attn_q_norm,
                     attn_k_norm, mlstm_out_norm, ab_w_out, ret_w_in, ret_decay_logit, ret_out_norm,
                     ret_w_out, ffn_w_up, ffn_conv_w, ffn_conv_b, ffn_w_down)
    return (y_prompt, y_sample)
```

```python
import functools

import numpy as np
import jax
import jax.numpy as jnp
from jax import lax
from jax.experimental import pallas as pl
from jax.experimental.pallas import tpu as pltpu

F32 = jnp.float32
BF16 = jnp.bfloat16

GRID_W = 64
ROPE_THETA = 10000.0
NORM_EPS = 1e-6
ATTN_HEADS = 8
ATTN_KV_HEADS = 2
ATTN_HEAD_DIM = 64
ATTN_Q_WIDTH = ATTN_HEADS * ATTN_HEAD_DIM
ATTN_KV_WIDTH = ATTN_KV_HEADS * ATTN_HEAD_DIM
MLSTM_HEADS = 4
MLSTM_HEAD_DIM = 128
MLSTM_WIDTH = MLSTM_HEADS * MLSTM_HEAD_DIM
MLSTM_GATES = 4 * MLSTM_HEADS
RET_HEADS = 4
RET_QK_DIM = 256
RET_V_DIM = 512
RET_QK_WIDTH = RET_HEADS * RET_QK_DIM
RET_V_WIDTH = RET_HEADS * RET_V_DIM
CHUNK = 128

LANES = 128
BF16_SUBLANES = 16
VMEM_LIMIT_BYTES = 56 * 1024 * 1024

ROW_TILE = 512
FFN_COL_TILE = 256
ATTN_K_TILE = 1024


def _cparams(n_axes):
    return pltpu.CompilerParams(
        dimension_semantics=("arbitrary",) * n_axes, vmem_limit_bytes=VMEM_LIMIT_BYTES)


def _rms(x, gain):
    ms = jnp.mean(x * x, axis=-1, keepdims=True)
    return x * lax.rsqrt(ms + NORM_EPS) * gain


def _dot(a, b):
    return jnp.dot(a, b, preferred_element_type=F32)


def _dot_nt(a, b):
    return lax.dot_general(a, b, (((1,), (1,)), ((), ())), preferred_element_type=F32)


def _dot_tn(a, b):
    return lax.dot_general(a, b, (((0,), (0,)), ((), ())), preferred_element_type=F32)


def _split3(x):
    x1 = x.astype(BF16)
    r1 = x - x1.astype(F32)
    x2 = r1.astype(BF16)
    x3 = (r1 - x2.astype(F32)).astype(BF16)
    return x1, x2, x3


def _log_sigmoid(x):
    return jnp.minimum(x, 0.0) - jnp.log1p(jnp.exp(-jnp.abs(x)))


def _ffn_kernel(xp_ref, x_ref, xn_ref, gain_ref, wu_ref, wg_ref, cw_ref, cb_ref, wd_ref, fin_ref,
                o_ref, h_ref, g_ref, acc_ref, *, tm, nf, blocks_per_seq, final):
    i = pl.program_id(0)
    halo = BF16_SUBLANES
    pos = i % blocks_per_seq
    gain = gain_ref[...]
    hp = jnp.where(pos == 0, 0.0, _rms(xp_ref[...], gain))
    hn = jnp.where(pos == blocks_per_seq - 1, 0.0, _rms(xn_ref[...], gain))
    x = x_ref[...]
    h_ref[0:halo, :] = hp.astype(BF16)
    h_ref[halo:halo + tm, :] = _rms(x, gain).astype(BF16)
    h_ref[halo + tm:, :] = hn.astype(BF16)
    acc_ref[...] = x
    sqrt_half = np.float32(np.sqrt(0.5))
    for j in range(nf):
        g_ref[...] = _dot(h_ref[...], wg_ref[j])
        cw = cw_ref[j]
        gc = (g_ref[halo - 1:halo - 1 + tm, :] * cw[0:1, :] + g_ref[halo:halo + tm, :] * cw[1:2, :]
              + g_ref[halo + 1:halo + 1 + tm, :] * cw[2:3, :] + cb_ref[j])
        u = _dot(h_ref[halo:halo + tm, :], wu_ref[j])
        act = 0.5 * gc * (1.0 + lax.erf(gc * sqrt_half))
        acc_ref[...] += _dot((act * u).astype(BF16), wd_ref[j])
    if final:
        o_ref[...] = _rms(acc_ref[...], fin_ref[...])
    else:
        o_ref[...] = acc_ref[...]


def _ffn(x, seq_len, gain, wu, wg, cw, cb, wd, fin_gain, final):
    t, d = x.shape
    tm = min(ROW_TILE, seq_len)
    nf, _, tf = wu.shape
    halo = BF16_SUBLANES
    nblk = t // tm
    hb = tm // halo
    const3 = lambda i: (0, 0, 0)
    kern = functools.partial(_ffn_kernel, tm=tm, nf=nf, blocks_per_seq=seq_len // tm, final=final)
    return pl.pallas_call(
        kern,
        grid=(nblk,),
        in_specs=[
            pl.BlockSpec((halo, d), lambda i: (jnp.maximum(i * hb - 1, 0), 0)),
            pl.BlockSpec((tm, d), lambda i: (i, 0)),
            pl.BlockSpec((halo, d), lambda i: (jnp.minimum((i + 1) * hb, nblk * hb - 1), 0)),
            pl.BlockSpec((1, d), lambda i: (0, 0)),
            pl.BlockSpec(wu.shape, const3, pipeline_mode=pl.Buffered(1)),
            pl.BlockSpec(wg.shape, const3, pipeline_mode=pl.Buffered(1)),
            pl.BlockSpec(cw.shape, const3),
            pl.BlockSpec(cb.shape, const3),
            pl.BlockSpec(wd.shape, const3, pipeline_mode=pl.Buffered(1)),
            pl.BlockSpec((1, d), lambda i: (0, 0)),
        ],
        out_specs=pl.BlockSpec((tm, d), lambda i: (i, 0)),
        out_shape=jax.ShapeDtypeStruct((t, d), F32),
        scratch_shapes=[
            pltpu.VMEM((tm + 2 * halo, d), BF16),
            pltpu.VMEM((tm + 2 * halo, tf), F32),
            pltpu.VMEM((tm, d), F32),
        ],
        compiler_params=_cparams(1),
        name="conv_ffn",
    )(x, x, x, gain, wu, wg, cw, cb, wd, fin_gain)


def _ret_proj_kernel(x_ref, gain_ref, w_ref, cos_ref, sin_ref, qk_ref, v_ref, g_ref, h_ref):
    j = pl.program_id(1)

    @pl.when(j == 0)
    def _():
        h_ref[...] = _rms(x_ref[...], gain_ref[...]).astype(BF16)

    acc = _dot(h_ref[...], w_ref[...])

    @pl.when(j < 2)
    def _():
        c = cos_ref[...]
        s = sin_ref[...]
        scale = jnp.where(j == 1, np.float32(RET_QK_DIM ** -0.5), np.float32(1.0))
        half = RET_QK_DIM // 2
        for hd in range(RET_HEADS):
            lo = hd * RET_QK_DIM
            x1 = acc[:, lo:lo + half]
            x2 = acc[:, lo + half:lo + 2 * half]
            qk_ref[:, lo:lo + half] = ((x1 * c - x2 * s) * scale).astype(BF16)
            qk_ref[:, lo + half:lo + 2 * half] = ((x2 * c + x1 * s) * scale).astype(BF16)

    @pl.when((j >= 2) & (j < 4))
    def _():
        v_ref[...] = acc.astype(BF16)

    @pl.when(j >= 4)
    def _():
        g_ref[...] = acc


def _ret_proj(x, seq_len, gain, w, cos, sin):
    t, d = x.shape
    tm = min(ROW_TILE, seq_len)
    tn = RET_QK_WIDTH
    sblk = seq_len // tm
    return pl.pallas_call(
        _ret_proj_kernel,
        grid=(t // tm, w.shape[1] // tn),
        in_specs=[
            pl.BlockSpec((tm, d), lambda i, j: (i, 0)),
            pl.BlockSpec((1, d), lambda i, j: (0, 0)),
            pl.BlockSpec((d, tn), lambda i, j: (0, j)),
            pl.BlockSpec((tm, LANES), lambda i, j: (i % sblk, 0)),
            pl.BlockSpec((tm, LANES), lambda i, j: (i % sblk, 0)),
        ],
        out_specs=[
            pl.BlockSpec((tm, tn), lambda i, j: (i, jnp.minimum(j, 1))),
            pl.BlockSpec((tm, tn), lambda i, j: (i, jnp.clip(j - 2, 0, 1))),
            pl.BlockSpec((tm, tn), lambda i, j: (i, jnp.clip(j - 4, 0, 1))),
        ],
        out_shape=[
            jax.ShapeDtypeStruct((t, 2 * RET_QK_WIDTH), BF16),
            jax.ShapeDtypeStruct((t, RET_V_WIDTH), BF16),
            jax.ShapeDtypeStruct((t, RET_V_WIDTH), F32),
        ],
        scratch_shapes=[pltpu.VMEM((tm, d), BF16)],
        compiler_params=_cparams(2),
        name="ret_proj",
    )(x, gain, w, cos, sin)


def _ret_kernel(qkf_ref, vf_ref, qkb_ref, vb_ref, intra_ref, qd_ref, kd_ref, cd_ref,
                yf_ref, yb_ref, s_ref):
    c = pl.program_id(1)

    @pl.when(c == 0)
    def _():
        s_ref[...] = jnp.zeros_like(s_ref)

    for d, (qk_ref, v_ref, y_ref) in enumerate(((qkf_ref, vf_ref, yf_ref), (qkb_ref, vb_ref, yb_ref))):
        for hd in range(RET_HEADS):
            idx = d * RET_HEADS + hd
            q = qk_ref[:, hd * RET_QK_DIM:(hd + 1) * RET_QK_DIM]
            k = qk_ref[:, RET_QK_WIDTH + hd * RET_QK_DIM:RET_QK_WIDTH + (hd + 1) * RET_QK_DIM]
            v = v_ref[:, hd * RET_V_DIM:(hd + 1) * RET_V_DIM]
            state = s_ref[idx]
            scores = _dot_nt(q, k) * intra_ref[d, hd]
            inter = _dot(q, state.astype(BF16)) * qd_ref[d, hd][:, 0:1]
            y_ref[:, hd * RET_V_DIM:(hd + 1) * RET_V_DIM] = _dot(scores.astype(BF16), v) + inter
            kdec = (k.astype(F32) * kd_ref[d, hd][:, 0:1]).astype(BF16)
            s_ref[idx] = cd_ref[d, hd][0:1, 0:1] * state + _dot_tn(kdec, v)


def _ret_core(qk, v, batch, seq_len, intra, qd, kd, cd):
    t = qk.shape[0]
    nc = seq_len // CHUNK
    fwd = lambda b, c: (b * nc + c, 0)
    bwd = lambda b, c: (b * nc + nc - 1 - c, 0)
    const4 = lambda b, c: (0, 0, 0, 0)
    return pl.pallas_call(
        _ret_kernel,
        grid=(batch, nc),
        in_specs=[
            pl.BlockSpec((CHUNK, 2 * RET_QK_WIDTH), fwd),
            pl.BlockSpec((CHUNK, RET_V_WIDTH), fwd),
            pl.BlockSpec((CHUNK, 2 * RET_QK_WIDTH), bwd),
            pl.BlockSpec((CHUNK, RET_V_WIDTH), bwd),
            pl.BlockSpec(intra.shape, const4),
            pl.BlockSpec(qd.shape, const4),
            pl.BlockSpec(kd.shape, const4),
            pl.BlockSpec(cd.shape, const4),
        ],
        out_specs=[
            pl.BlockSpec((CHUNK, RET_V_WIDTH), fwd),
            pl.BlockSpec((CHUNK, RET_V_WIDTH), bwd),
        ],
        out_shape=[jax.ShapeDtypeStruct((t, RET_V_WIDTH), F32)] * 2,
        scratch_shapes=[pltpu.VMEM((2 * RET_HEADS, RET_QK_DIM, RET_V_DIM), F32)],
        compiler_params=_cparams(2),
        name="retention",
    )(qk, v, qk, v, intra, qd, kd, cd)


def _ret_out_kernel(yf_ref, yb_ref, g_ref, gain_ref, w_ref, x_ref, o_ref):
    acc = x_ref[...]
    for hd in range(RET_HEADS):
        sl = slice(hd * RET_V_DIM, (hd + 1) * RET_V_DIM)
        y = _rms(yf_ref[:, sl] + yb_ref[:, sl], gain_ref[:, sl])
        g = g_ref[:, sl]
        z = (y * (g * jax.nn.sigmoid(g))).astype(BF16)
        acc = acc + _dot(z, w_ref[sl, :])
    o_ref[...] = acc


def _ret_out(yf, yb, g, gain, w, x, seq_len):
    t, d = x.shape
    tm = min(ROW_TILE, seq_len)
    row = lambda i: (i, 0)
    fixed = lambda i: (0, 0)
    return pl.pallas_call(
        _ret_out_kernel,
        grid=(t // tm,),
        in_specs=[
            pl.BlockSpec((tm, RET_V_WIDTH), row),
            pl.BlockSpec((tm, RET_V_WIDTH), row),
            pl.BlockSpec((tm, RET_V_WIDTH), row),
            pl.BlockSpec((1, RET_V_WIDTH), fixed),
            pl.BlockSpec((RET_V_WIDTH, d), fixed),
            pl.BlockSpec((tm, d), row),
        ],
        out_specs=pl.BlockSpec((tm, d), row),
        out_shape=jax.ShapeDtypeStruct((t, d), F32),
        compiler_params=_cparams(1),
        name="ret_out",
    )(yf, yb, g, gain, w, x)


def _ret_tables(decay_logit):
    log_gamma = jax.nn.log_sigmoid(decay_logit.astype(F32))
    pos = jnp.arange(CHUNK, dtype=F32)
    diff = pos[:, None] - pos[None, :]
    lg = log_gamma[:, :, None, None]
    tri = jnp.exp(jnp.maximum(diff, 0.0)[None, None] * lg)
    intra_f = jnp.where((diff >= 0)[None], tri[0], 0.0)
    intra_b = jnp.where((diff > 0)[None], tri[1], 0.0)
    intra_b = intra_b[:, ::-1, ::-1]
    qdec = jnp.exp((pos + 1.0)[None, None, :] * log_gamma[:, :, None])
    kdec = jnp.exp((CHUNK - 1.0 - pos)[None, None, :] * log_gamma[:, :, None])
    qdec = jnp.stack([qdec[0], qdec[1, :, ::-1]])
    kdec = jnp.stack([kdec[0], kdec[1, :, ::-1]])
    cdec = jnp.exp(CHUNK * log_gamma)
    rep = lambda a: jnp.broadcast_to(a[..., None], a.shape + (LANES,))
    return (jnp.stack([intra_f, intra_b]), rep(qdec), rep(kdec),
            jnp.broadcast_to(cdec[:, :, None, None], cdec.shape + (8, LANES)))


def _proj_plain_kernel(x_ref, gain_ref, w_ref, o_ref):
    o_ref[...] = _dot(_rms(x_ref[...], gain_ref[...]).astype(BF16), w_ref[...])


def _proj_plain(x, seq_len, gain, w):
    t, d = x.shape
    tm = min(ROW_TILE, seq_len)
    n = w.shape[1]
    return pl.pallas_call(
        _proj_plain_kernel,
        grid=(t // tm,),
        in_specs=[
            pl.BlockSpec((tm, d), lambda i: (i, 0)),
            pl.BlockSpec((1, d), lambda i: (0, 0)),
            pl.BlockSpec((d, n), lambda i: (0, 0)),
        ],
        out_specs=pl.BlockSpec((tm, n), lambda i: (i, 0)),
        out_shape=jax.ShapeDtypeStruct((t, n), F32),
        compiler_params=_cparams(1),
        name="ab_proj_attn",
    )(x, gain, w)


def _mlstm_proj_kernel(x_ref, gain_ref, w_ref, qkv_ref, o_ref, h_ref):
    j = pl.program_id(1)

    @pl.when(j == 0)
    def _():
        h_ref[...] = _rms(x_ref[...], gain_ref[...]).astype(BF16)

    acc = _dot(h_ref[...], w_ref[...])

    @pl.when(j < 3)
    def _():
        scale = jnp.where(j == 1, np.float32(MLSTM_HEAD_DIM ** -0.5), np.float32(1.0))
        qkv_ref[...] = (acc * scale).astype(BF16)

    @pl.when(j == 3)
    def _():
        o_ref[...] = acc


def _mlstm_proj(x, seq_len, gain, w):
    t, d = x.shape
    tm = min(ROW_TILE, seq_len)
    tn = MLSTM_WIDTH
    return pl.pallas_call(
        _mlstm_proj_kernel,
        grid=(t // tm, 4),
        in_specs=[
            pl.BlockSpec((tm, d), lambda i, j: (i, 0)),
            pl.BlockSpec((1, d), lambda i, j: (0, 0)),
            pl.BlockSpec((d, tn), lambda i, j: (0, j)),
        ],
        out_specs=[
            pl.BlockSpec((tm, tn), lambda i, j: (i, jnp.minimum(j, 2))),
            pl.BlockSpec((tm, tn), lambda i, j: (i, 0)),
        ],
        out_shape=[
            jax.ShapeDtypeStruct((t, 3 * MLSTM_WIDTH), BF16),
            jax.ShapeDtypeStruct((t, MLSTM_WIDTH), F32),
        ],
        scratch_shapes=[pltpu.VMEM((tm, d), BF16)],
        compiler_params=_cparams(2),
        name="ab_proj_mlstm",
    )(x, gain, w)


def _attn_prep_kernel(pa_ref, cos_ref, sin_ref, qg_ref, kg_ref, q_ref, kt_ref, v_ref, *, tm):
    lane = lax.broadcasted_iota(jnp.int32, (tm, LANES), 1)
    ri = lax.broadcasted_iota(jnp.int32, (LANES, LANES), 0)
    ci = lax.broadcasted_iota(jnp.int32, (LANES, LANES), 1)
    same_head = ((ri // ATTN_HEAD_DIM) == (ci // ATTN_HEAD_DIM)).astype(BF16)
    c = cos_ref[...]
    s = sin_ref[...]
    first_half = (lane % ATTN_HEAD_DIM) < (ATTN_HEAD_DIM // 2)
    low = lane < ATTN_HEAD_DIM

    def norm_rope(x, gain):
        x2 = x * x
        hi = x2.astype(BF16)
        lo = (x2 - hi.astype(F32)).astype(BF16)
        ssum = _dot(hi, same_head) + _dot(lo, same_head)
        xn = x * lax.rsqrt(ssum * np.float32(1.0 / ATTN_HEAD_DIM) + NORM_EPS) * gain
        swapped = jnp.where(first_half, pltpu.roll(xn, LANES - ATTN_HEAD_DIM // 2, 1),
                            pltpu.roll(xn, ATTN_HEAD_DIM // 2, 1))
        return xn * c + swapped * s

    for p in range(ATTN_HEADS // 2):
        y = norm_rope(pa_ref[:, p * LANES:(p + 1) * LANES], qg_ref[...]) * np.float32(ATTN_HEAD_DIM ** -0.5)
        yr = pltpu.roll(y, ATTN_HEAD_DIM, 1)
        if (2 * p) // (ATTN_HEADS // ATTN_KV_HEADS) == 0:
            head_a, head_b = jnp.where(low, y, 0.0), jnp.where(low, yr, 0.0)
        else:
            head_a, head_b = jnp.where(low, 0.0, yr), jnp.where(low, 0.0, y)
        q_ref[0, 2 * p] = head_a.astype(BF16)
        q_ref[0, 2 * p + 1] = head_b.astype(BF16)
    k = norm_rope(pa_ref[:, ATTN_Q_WIDTH:ATTN_Q_WIDTH + LANES], kg_ref[...])
    kt_ref[0] = k.T.astype(BF16)
    v_ref[0] = pa_ref[:, ATTN_Q_WIDTH + LANES:ATTN_Q_WIDTH + 2 * LANES].astype(BF16)


def _attn_prep(pa, batch, seq_len, cos, sin, qg, kg):
    tm = min(ROW_TILE, seq_len)
    sblk = seq_len // tm
    n = pa.shape[1]
    return pl.pallas_call(
        functools.partial(_attn_prep_kernel, tm=tm),
        grid=(batch, sblk),
        in_specs=[
            pl.BlockSpec((tm, n), lambda b, s: (b * sblk + s, 0)),
            pl.BlockSpec((tm, LANES), lambda b, s: (s, 0)),
            pl.BlockSpec((tm, LANES), lambda b, s: (s, 0)),
            pl.BlockSpec((1, LANES), lambda b, s: (0, 0)),
            pl.BlockSpec((1, LANES), lambda b, s: (0, 0)),
        ],
        out_specs=[
            pl.BlockSpec((1, ATTN_HEADS, tm, LANES), lambda b, s: (b, 0, s, 0)),
            pl.BlockSpec((1, LANES, tm), lambda b, s: (b, 0, s)),
            pl.BlockSpec((1, tm, LANES), lambda b, s: (b, s, 0)),
        ],
        out_shape=[
            jax.ShapeDtypeStruct((batch, ATTN_HEADS, seq_len, LANES), BF16),
            jax.ShapeDtypeStruct((batch, LANES, seq_len), BF16),
            jax.ShapeDtypeStruct((batch, seq_len, LANES), BF16),
        ],
        compiler_params=_cparams(2),
        name="attn_prep",
    )(pa, cos, sin, qg, kg)


def _attn_kernel(q_ref, kt_ref, v_ref, o_ref, m_ref, l_ref, acc_ref, *, tq, nk):
    kk = pl.program_id(2)

    @pl.when(kk == 0)
    def _():
        m_ref[...] = jnp.full_like(m_ref, -jnp.inf)
        l_ref[...] = jnp.zeros_like(l_ref)
        acc_ref[...] = jnp.zeros_like(acc_ref)

    q = q_ref[0].reshape(ATTN_HEADS * tq, LANES)
    s = _dot(q, kt_ref[0])
    m_prev = m_ref[...]
    m_new = jnp.maximum(m_prev, jnp.max(s, axis=-1, keepdims=True))
    alpha = jnp.exp(m_prev - m_new)
    p = jnp.exp(s - m_new)
    l_ref[...] = alpha * l_ref[...] + jnp.sum(p, axis=-1, keepdims=True)
    acc_ref[...] = alpha * acc_ref[...] + _dot(p.astype(BF16), v_ref[0])
    m_ref[...] = m_new

    @pl.when(kk == nk - 1)
    def _():
        o = acc_ref[...] / l_ref[...]
        lane = lax.broadcasted_iota(jnp.int32, (tq, LANES), 1)
        low = lane < ATTN_HEAD_DIM
        for p2 in range(ATTN_HEADS // 2):
            a = o[(2 * p2) * tq:(2 * p2 + 1) * tq]
            b = o[(2 * p2 + 1) * tq:(2 * p2 + 2) * tq]
            if (2 * p2) // (ATTN_HEADS // ATTN_KV_HEADS) == 0:
                pair = jnp.where(low, a, pltpu.roll(b, ATTN_HEAD_DIM, 1))
            else:
                pair = jnp.where(low, pltpu.roll(a, ATTN_HEAD_DIM, 1), b)
            o_ref[0, :, p2 * LANES:(p2 + 1) * LANES] = pair.astype(BF16)


def _attention(q, kt, v, batch, seq_len):
    tq = CHUNK
    tk = min(ATTN_K_TILE, seq_len)
    nk = seq_len // tk
    return pl.pallas_call(
        functools.partial(_attn_kernel, tq=tq, nk=nk),
        grid=(batch, seq_len // tq, nk),
        in_specs=[
            pl.BlockSpec((1, ATTN_HEADS, tq, LANES), lambda b, i, k: (b, 0, i, 0)),
            pl.BlockSpec((1, LANES, tk), lambda b, i, k: (b, 0, k)),
            pl.BlockSpec((1, tk, LANES), lambda b, i, k: (b, k, 0)),
        ],
        out_specs=pl.BlockSpec((1, tq, ATTN_Q_WIDTH), lambda b, i, k: (b, i, 0)),
        out_shape=jax.ShapeDtypeStruct((batch, seq_len, ATTN_Q_WIDTH), BF16),
        scratch_shapes=[
            pltpu.VMEM((ATTN_HEADS * tq, 1), F32),
            pltpu.VMEM((ATTN_HEADS * tq, 1), F32),
            pltpu.VMEM((ATTN_HEADS * tq, LANES), F32),
        ],
        compiler_params=_cparams(3),
        name="attention",
    )(q, kt, v)


def _mlstm_kernel(qkvf_ref, gf_ref, qkvb_ref, gb_ref, bias_ref, hf_ref, hb_ref, c_ref, n_ref, m_ref):
    c = pl.program_id(1)

    @pl.when(c == 0)
    def _():
        c_ref[...] = jnp.zeros_like(c_ref)
        n_ref[...] = jnp.zeros_like(n_ref)
        m_ref[...] = jnp.zeros_like(m_ref)

    L = CHUNK
    dh = MLSTM_HEAD_DIM
    row = lax.broadcasted_iota(jnp.int32, (L, L), 0)
    col = lax.broadcasted_iota(jnp.int32, (L, L), 1)
    for d, (qkv_ref, g_ref, h_ref) in enumerate(((qkvf_ref, gf_ref, hf_ref), (qkvb_ref, gb_ref, hb_ref))):
        allowed = (col <= row) if d == 0 else (col >= row)
        gates = g_ref[...] + bias_ref[...]
        gates_t = gates.T
        logf = _log_sigmoid(gates)
        logf_t = _log_sigmoid(gates_t[0:MLSTM_GATES, :])
        sel_col = allowed.astype(BF16)
        sel_row = ((row <= col) if d == 0 else (row >= col)).astype(BF16)
        f1, f2, f3 = _split3(logf)
        bcum = _dot(sel_col, f1) + _dot(sel_col, f2) + _dot(sel_col, f3)
        t1, t2, t3 = _split3(logf_t)
        bcum_t = _dot(t1, sel_row) + _dot(t2, sel_row) + _dot(t3, sel_row)
        for hd in range(MLSTM_HEADS):
            idx = d * MLSTM_HEADS + hd
            ig = 2 * MLSTM_HEADS * d + hd
            fg = ig + MLSTM_HEADS
            q = qkv_ref[:, hd * dh:(hd + 1) * dh]
            k = qkv_ref[:, MLSTM_WIDTH + hd * dh:MLSTM_WIDTH + (hd + 1) * dh]
            v = qkv_ref[:, 2 * MLSTM_WIDTH + hd * dh:2 * MLSTM_WIDTH + (hd + 1) * dh]
            i_row = gates_t[ig:ig + 1, :]
            f_row = logf_t[fg:fg + 1, :]
            b_row = bcum_t[fg:fg + 1, :]
            i_col = jnp.sum(jnp.where(col == ig, gates, 0.0), axis=-1, keepdims=True)
            b_col = jnp.sum(jnp.where(col == fg, bcum, 0.0), axis=-1, keepdims=True)
            m_old = m_ref[idx][:, 0:1]
            n_old = n_ref[idx]
            c_old = c_ref[idx]
            a_row = i_row - b_row
            dlog = jnp.where(allowed, b_col + a_row, -jnp.inf)
            inter = b_col + m_old
            m_row = jnp.maximum(inter, jnp.max(dlog, axis=-1, keepdims=True))
            w = jnp.exp(dlog - m_row) * _dot_nt(q, k)
            inter_w = jnp.exp(inter - m_row)
            num = _dot(w.astype(BF16), v) + inter_w * _dot(q, c_old.astype(BF16))
            qn = jnp.sum(q.astype(F32) * n_old, axis=-1, keepdims=True)
            den = jnp.sum(w, axis=-1, keepdims=True) + inter_w * qn
            h_ref[:, hd * dh:(hd + 1) * dh] = num / jnp.maximum(jnp.abs(den), jnp.exp(-m_row))
            b_last = jnp.sum(f_row, axis=-1, keepdims=True)
            m_new = jnp.maximum(b_last + m_old, jnp.max(b_last + a_row, axis=-1, keepdims=True))
            keep = jnp.exp(b_last + m_old - m_new)
            kw = k.astype(F32) * jnp.exp(b_last + (i_col - b_col) - m_new)
            c_ref[idx] = keep * c_old + _dot_tn(kw.astype(BF16), v)
            n_ref[idx] = keep * n_old + jnp.sum(kw, axis=0, keepdims=True)
            m_ref[idx] = jnp.broadcast_to(m_new, (1, LANES))


def _mlstm(qkv, pa, bias, batch, seq_len):
    t = qkv.shape[0]
    nc = seq_len // CHUNK
    gate_blk = (ATTN_Q_WIDTH + 2 * ATTN_KV_WIDTH) // LANES
    fwd = lambda b, c: (b * nc + c, 0)
    bwd = lambda b, c: (b * nc + nc - 1 - c, 0)
    nstate = 2 * MLSTM_HEADS
    return pl.pallas_call(
        _mlstm_kernel,
        grid=(batch, nc),
        in_specs=[
            pl.BlockSpec((CHUNK, 3 * MLSTM_WIDTH), fwd),
            pl.BlockSpec((CHUNK, LANES), lambda b, c: (b * nc + c, gate_blk)),
            pl.BlockSpec((CHUNK, 3 * MLSTM_WIDTH), bwd),
            pl.BlockSpec((CHUNK, LANES), lambda b, c: (b * nc + nc - 1 - c, gate_blk)),
            pl.BlockSpec((1, LANES), lambda b, c: (0, 0)),
        ],
        out_specs=[
            pl.BlockSpec((CHUNK, MLSTM_WIDTH), fwd),
            pl.BlockSpec((CHUNK, MLSTM_WIDTH), bwd),
        ],
        out_shape=[jax.ShapeDtypeStruct((t, MLSTM_WIDTH), F32)] * 2,
        scratch_shapes=[
            pltpu.VMEM((nstate, MLSTM_HEAD_DIM, MLSTM_HEAD_DIM), F32),
            pltpu.VMEM((nstate, 1, MLSTM_HEAD_DIM), F32),
            pltpu.VMEM((nstate, 1, LANES), F32),
        ],
        compiler_params=_cparams(2),
        name="mlstm",
    )(qkv, pa, qkv, pa, bias)


def _ab_out_kernel(a_ref, hf_ref, hb_ref, mo_ref, gain_ref, w_ref, x_ref, o_ref):
    acc = x_ref[...] + _dot(a_ref[...], w_ref[0:ATTN_Q_WIDTH, :])
    for hd in range(MLSTM_HEADS):
        sl = slice(hd * MLSTM_HEAD_DIM, (hd + 1) * MLSTM_HEAD_DIM)
        y = _rms(hf_ref[:, sl] + hb_ref[:, sl], gain_ref[:, sl])
        z = (y * jax.nn.sigmoid(mo_ref[:, sl])).astype(BF16)
        acc = acc + _dot(z, w_ref[ATTN_Q_WIDTH + hd * MLSTM_HEAD_DIM:ATTN_Q_WIDTH + (hd + 1) * MLSTM_HEAD_DIM, :])
    o_ref[...] = acc


def _ab_out(attn, hf, hb, mo, gain, w, x, seq_len):
    t, d = x.shape
    tm = min(ROW_TILE, seq_len)
    row = lambda i: (i, 0)
    fixed = lambda i: (0, 0)
    return pl.pallas_call(
        _ab_out_kernel,
        grid=(t // tm,),
        in_specs=[
            pl.BlockSpec((tm, ATTN_Q_WIDTH), row),
            pl.BlockSpec((tm, MLSTM_WIDTH), row),
            pl.BlockSpec((tm, MLSTM_WIDTH), row),
            pl.BlockSpec((tm, MLSTM_WIDTH), row),
            pl.BlockSpec((1, MLSTM_WIDTH), fixed),
            pl.BlockSpec(w.shape, fixed),
            pl.BlockSpec((tm, d), row),
        ],
        out_specs=pl.BlockSpec((tm, d), row),
        out_shape=jax.ShapeDtypeStruct((t, d), F32),
        compiler_params=_cparams(1),
        name="ab_out",
    )(attn, hf, hb, mo, gain, w, x)


def _rope_angles(seq_len, head_dim):
    rows = seq_len // GRID_W
    row_idx = jnp.repeat(jnp.arange(rows, dtype=F32), GRID_W)
    col_idx = jnp.tile(jnp.arange(GRID_W, dtype=F32), rows)
    axis_dim = head_dim // 2
    inv_freq = ROPE_THETA ** (-jnp.arange(0, axis_dim, 2, dtype=F32) / axis_dim)
    ang = jnp.concatenate([row_idx[:, None] * inv_freq, col_idx[:, None] * inv_freq], axis=-1)
    return jnp.cos(ang), jnp.sin(ang)


def _prepare(p):
    d = p["ab_w_in"].shape[1]
    n_ab = p["ab_w_in"].shape[0]
    a_cols = ATTN_Q_WIDTH + 2 * ATTN_KV_WIDTH
    m_lo = a_cols
    m_hi = a_cols + 4 * MLSTM_WIDTH
    w_in = p["ab_w_in"]
    pad = jnp.zeros((n_ab, d, LANES - MLSTM_GATES), w_in.dtype)
    out = dict(
        ab_w_attn=jnp.concatenate([w_in[:, :, :a_cols], w_in[:, :, m_hi:], pad], axis=-1).astype(BF16),
        ab_w_mlstm=w_in[:, :, m_lo:m_hi].astype(BF16),
        ab_bias=jnp.pad(p["ab_gate_bias"].astype(F32), ((0, 0), (0, LANES - MLSTM_GATES)))[:, None, :],
        ab_w_out=p["ab_w_out"].astype(BF16),
        attn_qg=jnp.tile(p["attn_q_norm"].astype(F32), (1, LANES // ATTN_HEAD_DIM))[:, None, :],
        attn_kg=jnp.tile(p["attn_k_norm"].astype(F32), (1, LANES // ATTN_HEAD_DIM))[:, None, :],
        mlstm_gain=p["mlstm_out_norm"].astype(F32)[:, None, :],
        ret_w_in=p["ret_w_in"].astype(BF16),
        ret_w_out=p["ret_w_out"].astype(BF16),
        ret_gain=p["ret_out_norm"].astype(F32)[:, None, :],
        ret_tables=[_ret_tables(p["ret_decay_logit"][j]) for j in range(p["ret_w_in"].shape[0])],
        norm_mix=p["norm_mix"].astype(F32)[:, None, :],
        norm_ffn=p["norm_ffn"].astype(F32)[:, None, :],
        norm_final=p["norm_final"].astype(F32)[None, :],
    )
    depth, _, ff2 = p["ffn_w_up"].shape
    dff = ff2 // 2
    nf = dff // FFN_COL_TILE
    up = p["ffn_w_up"].astype(BF16)
    tiles = lambda w: w.reshape(depth, d, nf, FFN_COL_TILE).transpose(0, 2, 1, 3)
    out["ffn_wu"] = tiles(up[:, :, :dff])
    out["ffn_wg"] = tiles(up[:, :, dff:])
    out["ffn_wd"] = p["ffn_w_down"].astype(BF16).reshape(depth, nf, FFN_COL_TILE, d)
    cw = p["ffn_conv_w"].astype(F32).reshape(depth, 3, nf, FFN_COL_TILE).transpose(0, 2, 1, 3)
    out["ffn_cw"] = jnp.pad(cw, ((0, 0), (0, 0), (0, 5), (0, 0)))
    out["ffn_cb"] = p["ffn_conv_b"].astype(F32).reshape(depth, nf, 1, FFN_COL_TILE)
    return out


def _trunk(x3, w):
    batch, seq_len, d = x3.shape
    x = x3.reshape(batch * seq_len, d)
    cos_a, sin_a = _rope_angles(seq_len, ATTN_HEAD_DIM)
    reps = LANES // (ATTN_HEAD_DIM // 2)
    cos_a = jnp.tile(cos_a, (1, reps))
    sin_a = jnp.tile(jnp.concatenate([-sin_a, sin_a], axis=-1), (1, reps // 2))
    cos_r, sin_r = _rope_angles(seq_len, RET_QK_DIM)
    depth = w["norm_mix"].shape[0]
    for layer in range(depth):
        j = layer // 2
        if layer % 2 == 0:
            pa = _proj_plain(x, seq_len, w["norm_mix"][layer], w["ab_w_attn"][j])
            qkv, mo = _mlstm_proj(x, seq_len, w["norm_mix"][layer], w["ab_w_mlstm"][j])
            q, kt, v = _attn_prep(pa, batch, seq_len, cos_a, sin_a, w["attn_qg"][j], w["attn_kg"][j])
            attn = _attention(q, kt, v, batch, seq_len).reshape(batch * seq_len, ATTN_Q_WIDTH)
            hf, hb = _mlstm(qkv, pa, w["ab_bias"][j], batch, seq_len)
            x = _ab_out(attn, hf, hb, mo, w["mlstm_gain"][j], w["ab_w_out"][j], x, seq_len)
        else:
            qk, v, g = _ret_proj(x, seq_len, w["norm_mix"][layer], w["ret_w_in"][j], cos_r, sin_r)
            yf, yb = _ret_core(qk, v, batch, seq_len, *w["ret_tables"][j])
            x = _ret_out(yf, yb, g, w["ret_gain"][j], w["ret_w_out"][j], x, seq_len)
        x = _ffn(x, seq_len, w["norm_ffn"][layer], w["ffn_wu"][layer], w["ffn_wg"][layer], w["ffn_cw"][layer],
                 w["ffn_cb"][layer], w["ffn_wd"][layer], w["norm_final"], final=(layer == depth - 1))
    return x.reshape(batch, seq_len, d)


def kernel(x_prompt, x_sample, norm_mix, norm_ffn, norm_final, ab_w_in, ab_gate_bias, attn_q_norm, attn_k_norm,
           mlstm_out_norm, ab_w_out, ret_w_in, ret_decay_logit, ret_out_norm, ret_w_out, ffn_w_up, ffn_conv_w,
           ffn_conv_b, ffn_w_down):
    w = _prepare(dict(
        norm_mix=norm_mix, norm_ffn=norm_ffn, norm_final=norm_final, ab_w_in=ab_w_in, ab_gate_bias=ab_gate_bias,
        attn_q_norm=attn_q_norm, attn_k_norm=attn_k_norm, mlstm_out_norm=mlstm_out_norm, ab_w_out=ab_w_out,
        ret_w_in=ret_w_in, ret_decay_logit=ret_decay_logit, ret_out_norm=ret_out_norm, ret_w_out=ret_w_out,
        ffn_w_up=ffn_w_up, ffn_conv_w=ffn_conv_w, ffn_conv_b=ffn_conv_b, ffn_w_down=ffn_w_down))
    return (_trunk(x_prompt, w), _trunk(x_sample, w))
```

```python
import functools

import numpy as np
import jax
import jax.numpy as jnp
from jax import lax
from jax.experimental import pallas as pl
from jax.experimental.pallas import tpu as pltpu

F32 = jnp.float32
BF16 = jnp.bfloat16

GRID_W = 64
ROPE_THETA = 10000.0
NORM_EPS = 1e-6
ATTN_HEADS = 8
ATTN_KV_HEADS = 2
ATTN_HEAD_DIM = 64
ATTN_Q_WIDTH = ATTN_HEADS * ATTN_HEAD_DIM
ATTN_KV_WIDTH = ATTN_KV_HEADS * ATTN_HEAD_DIM
MLSTM_HEADS = 4
MLSTM_HEAD_DIM = 128
MLSTM_WIDTH = MLSTM_HEADS * MLSTM_HEAD_DIM
MLSTM_GATES = 4 * MLSTM_HEADS
RET_HEADS = 4
RET_QK_DIM = 256
RET_V_DIM = 512
RET_QK_WIDTH = RET_HEADS * RET_QK_DIM
RET_V_WIDTH = RET_HEADS * RET_V_DIM
CHUNK = 128
RET_CHUNK = 256

LANES = 128
BF16_SUBLANES = 16
VMEM_LIMIT_BYTES = 56 * 1024 * 1024

ROW_TILE = 512
FFN_COL_TILE = 256
ATTN_K_TILE = 1024
MLSTM_BATCH_BLOCK = 4


def _cparams(n_axes):
    return pltpu.CompilerParams(
        dimension_semantics=("arbitrary",) * n_axes, vmem_limit_bytes=VMEM_LIMIT_BYTES)


def _rms(x, gain):
    ms = jnp.mean(x * x, axis=-1, keepdims=True)
    return x * lax.rsqrt(ms + NORM_EPS) * gain


def _dot(a, b):
    return jnp.dot(a, b, preferred_element_type=F32)


def _dot_nt(a, b):
    return lax.dot_general(a, b, (((1,), (1,)), ((), ())), preferred_element_type=F32)


def _dot_tn(a, b):
    return lax.dot_general(a, b, (((0,), (0,)), ((), ())), preferred_element_type=F32)


def _split3(x):
    x1 = x.astype(BF16)
    r1 = x - x1.astype(F32)
    x2 = r1.astype(BF16)
    x3 = (r1 - x2.astype(F32)).astype(BF16)
    return x1, x2, x3


def _log_sigmoid(x):
    return jnp.minimum(x, 0.0) - jnp.log1p(jnp.exp(-jnp.abs(x)))


def _ffn_kernel(xp_ref, x_ref, xn_ref, gain_ref, wu_ref, wg_ref, cw_ref, cb_ref, wd_ref, fin_ref,
                o_ref, h_ref, g_ref, acc_ref, *, tm, nf, blocks_per_seq, final):
    i = pl.program_id(0)
    halo = BF16_SUBLANES
    pos = i % blocks_per_seq
    gain = gain_ref[...]
    hp = jnp.where(pos == 0, 0.0, _rms(xp_ref[...], gain))
    hn = jnp.where(pos == blocks_per_seq - 1, 0.0, _rms(xn_ref[...], gain))
    x = x_ref[...]
    h_ref[0:halo, :] = hp.astype(BF16)
    h_ref[halo:halo + tm, :] = _rms(x, gain).astype(BF16)
    h_ref[halo + tm:, :] = hn.astype(BF16)
    acc_ref[...] = x
    sqrt_half = np.float32(np.sqrt(0.5))
    for j in range(nf):
        g_ref[...] = _dot(h_ref[...], wg_ref[j])
        cw = cw_ref[j]
        gc = (g_ref[halo - 1:halo - 1 + tm, :] * cw[0:1, :] + g_ref[halo:halo + tm, :] * cw[1:2, :]
              + g_ref[halo + 1:halo + 1 + tm, :] * cw[2:3, :] + cb_ref[j])
        u = _dot(h_ref[halo:halo + tm, :], wu_ref[j])
        act = 0.5 * gc * (1.0 + lax.erf(gc * sqrt_half))
        acc_ref[...] += _dot((act * u).astype(BF16), wd_ref[j])
    if final:
        o_ref[...] = _rms(acc_ref[...], fin_ref[...])
    else:
        o_ref[...] = acc_ref[...]


def _ffn(x, seq_len, gain, wu, wg, cw, cb, wd, fin_gain, final):
    t, d = x.shape
    tm = min(ROW_TILE, seq_len)
    nf, _, tf = wu.shape
    halo = BF16_SUBLANES
    nblk = t // tm
    hb = tm // halo
    const3 = lambda i: (0, 0, 0)
    kern = functools.partial(_ffn_kernel, tm=tm, nf=nf, blocks_per_seq=seq_len // tm, final=final)
    return pl.pallas_call(
        kern,
        grid=(nblk,),
        in_specs=[
            pl.BlockSpec((halo, d), lambda i: (jnp.maximum(i * hb - 1, 0), 0)),
            pl.BlockSpec((tm, d), lambda i: (i, 0)),
            pl.BlockSpec((halo, d), lambda i: (jnp.minimum((i + 1) * hb, nblk * hb - 1), 0)),
            pl.BlockSpec((1, d), lambda i: (0, 0)),
            pl.BlockSpec(wu.shape, const3, pipeline_mode=pl.Buffered(1)),
            pl.BlockSpec(wg.shape, const3, pipeline_mode=pl.Buffered(1)),
            pl.BlockSpec(cw.shape, const3),
            pl.BlockSpec(cb.shape, const3),
            pl.BlockSpec(wd.shape, const3, pipeline_mode=pl.Buffered(1)),
            pl.BlockSpec((1, d), lambda i: (0, 0)),
        ],
        out_specs=pl.BlockSpec((tm, d), lambda i: (i, 0)),
        out_shape=jax.ShapeDtypeStruct((t, d), F32),
        scratch_shapes=[
            pltpu.VMEM((tm + 2 * halo, d), BF16),
            pltpu.VMEM((tm + 2 * halo, tf), F32),
            pltpu.VMEM((tm, d), F32),
        ],
        compiler_params=_cparams(1),
        name="conv_ffn",
    )(x, x, x, gain, wu, wg, cw, cb, wd, fin_gain)


def _ret_proj_kernel(x_ref, gain_ref, w_ref, cos_ref, sin_ref, qk_ref, v_ref, g_ref, h_ref):
    j = pl.program_id(1)

    @pl.when(j == 0)
    def _():
        h_ref[...] = _rms(x_ref[...], gain_ref[...]).astype(BF16)

    acc = _dot(h_ref[...], w_ref[...])

    @pl.when(j < 2)
    def _():
        c = cos_ref[...]
        s = sin_ref[...]
        scale = jnp.where(j == 1, np.float32(RET_QK_DIM ** -0.5), np.float32(1.0))
        half = RET_QK_DIM // 2
        for hd in range(RET_HEADS):
            lo = hd * RET_QK_DIM
            x1 = acc[:, lo:lo + half]
            x2 = acc[:, lo + half:lo + 2 * half]
            qk_ref[:, lo:lo + half] = ((x1 * c - x2 * s) * scale).astype(BF16)
            qk_ref[:, lo + half:lo + 2 * half] = ((x2 * c + x1 * s) * scale).astype(BF16)

    @pl.when((j >= 2) & (j < 4))
    def _():
        v_ref[...] = acc.astype(BF16)

    @pl.when(j >= 4)
    def _():
        g_ref[...] = acc


def _ret_proj(x, seq_len, gain, w, cos, sin):
    t, d = x.shape
    tm = min(ROW_TILE, seq_len)
    tn = RET_QK_WIDTH
    sblk = seq_len // tm
    return pl.pallas_call(
        _ret_proj_kernel,
        grid=(t // tm, w.shape[1] // tn),
        in_specs=[
            pl.BlockSpec((tm, d), lambda i, j: (i, 0)),
            pl.BlockSpec((1, d), lambda i, j: (0, 0)),
            pl.BlockSpec((d, tn), lambda i, j: (0, j)),
            pl.BlockSpec((tm, LANES), lambda i, j: (i % sblk, 0)),
            pl.BlockSpec((tm, LANES), lambda i, j: (i % sblk, 0)),
        ],
        out_specs=[
            pl.BlockSpec((tm, tn), lambda i, j: (i, jnp.minimum(j, 1))),
            pl.BlockSpec((tm, tn), lambda i, j: (i, jnp.clip(j - 2, 0, 1))),
            pl.BlockSpec((tm, tn), lambda i, j: (i, jnp.clip(j - 4, 0, 1))),
        ],
        out_shape=[
            jax.ShapeDtypeStruct((t, 2 * RET_QK_WIDTH), BF16),
            jax.ShapeDtypeStruct((t, RET_V_WIDTH), BF16),
            jax.ShapeDtypeStruct((t, RET_V_WIDTH), F32),
        ],
        scratch_shapes=[pltpu.VMEM((tm, d), BF16)],
        compiler_params=_cparams(2),
        name="ret_proj",
    )(x, gain, w, cos, sin)


def _ret_kernel(qkf_ref, vf_ref, qkb_ref, vb_ref, intra_ref, qd_ref, kd_ref, cd_ref,
                yf_ref, yb_ref, s_ref):
    c = pl.program_id(1)

    @pl.when(c == 0)
    def _():
        s_ref[...] = jnp.zeros_like(s_ref)

    for d, (qk_ref, v_ref, y_ref) in enumerate(((qkf_ref, vf_ref, yf_ref), (qkb_ref, vb_ref, yb_ref))):
        for hd in range(RET_HEADS):
            idx = d * RET_HEADS + hd
            q = qk_ref[:, hd * RET_QK_DIM:(hd + 1) * RET_QK_DIM]
            k = qk_ref[:, RET_QK_WIDTH + hd * RET_QK_DIM:RET_QK_WIDTH + (hd + 1) * RET_QK_DIM]
            v = v_ref[:, hd * RET_V_DIM:(hd + 1) * RET_V_DIM]
            state = s_ref[idx]
            scores = _dot_nt(q, k) * intra_ref[d, hd]
            inter = _dot(q, state.astype(BF16)) * qd_ref[d, hd][:, 0:1]
            y_ref[:, hd * RET_V_DIM:(hd + 1) * RET_V_DIM] = _dot(scores.astype(BF16), v) + inter
            kdec = (k.astype(F32) * kd_ref[d, hd][:, 0:1]).astype(BF16)
            s_ref[idx] = cd_ref[d, hd][0:1, 0:1] * state + _dot_tn(kdec, v)


def _ret_core(qk, v, batch, seq_len, intra, qd, kd, cd):
    t = qk.shape[0]
    chunk = intra.shape[-1]
    nc = seq_len // chunk
    fwd = lambda b, c: (b * nc + c, 0)
    bwd = lambda b, c: (b * nc + nc - 1 - c, 0)
    const4 = lambda b, c: (0, 0, 0, 0)
    return pl.pallas_call(
        _ret_kernel,
        grid=(batch, nc),
        in_specs=[
            pl.BlockSpec((chunk, 2 * RET_QK_WIDTH), fwd),
            pl.BlockSpec((chunk, RET_V_WIDTH), fwd),
            pl.BlockSpec((chunk, 2 * RET_QK_WIDTH), bwd),
            pl.BlockSpec((chunk, RET_V_WIDTH), bwd),
            pl.BlockSpec(intra.shape, const4),
            pl.BlockSpec(qd.shape, const4),
            pl.BlockSpec(kd.shape, const4),
            pl.BlockSpec(cd.shape, const4),
        ],
        out_specs=[
            pl.BlockSpec((chunk, RET_V_WIDTH), fwd),
            pl.BlockSpec((chunk, RET_V_WIDTH), bwd),
        ],
        out_shape=[jax.ShapeDtypeStruct((t, RET_V_WIDTH), F32)] * 2,
        scratch_shapes=[pltpu.VMEM((2 * RET_HEADS, RET_QK_DIM, RET_V_DIM), F32)],
        compiler_params=_cparams(2),
        name="retention",
    )(qk, v, qk, v, intra, qd, kd, cd)


def _ret_out_kernel(yf_ref, yb_ref, g_ref, gain_ref, w_ref, x_ref, o_ref):
    acc = x_ref[...]
    for hd in range(RET_HEADS):
        sl = slice(hd * RET_V_DIM, (hd + 1) * RET_V_DIM)
        y = _rms(yf_ref[:, sl] + yb_ref[:, sl], gain_ref[:, sl])
        g = g_ref[:, sl]
        z = (y * (g * jax.nn.sigmoid(g))).astype(BF16)
        acc = acc + _dot(z, w_ref[sl, :])
    o_ref[...] = acc


def _ret_out(yf, yb, g, gain, w, x, seq_len):
    t, d = x.shape
    tm = min(ROW_TILE, seq_len)
    row = lambda i: (i, 0)
    fixed = lambda i: (0, 0)
    return pl.pallas_call(
        _ret_out_kernel,
        grid=(t // tm,),
        in_specs=[
            pl.BlockSpec((tm, RET_V_WIDTH), row),
            pl.BlockSpec((tm, RET_V_WIDTH), row),
            pl.BlockSpec((tm, RET_V_WIDTH), row),
            pl.BlockSpec((1, RET_V_WIDTH), fixed),
            pl.BlockSpec((RET_V_WIDTH, d), fixed),
            pl.BlockSpec((tm, d), row),
        ],
        out_specs=pl.BlockSpec((tm, d), row),
        out_shape=jax.ShapeDtypeStruct((t, d), F32),
        compiler_params=_cparams(1),
        name="ret_out",
    )(yf, yb, g, gain, w, x)


def _ret_tables(decay_logit, chunk):
    log_gamma = jax.nn.log_sigmoid(decay_logit.astype(F32))
    pos = jnp.arange(chunk, dtype=F32)
    diff = pos[:, None] - pos[None, :]
    lg = log_gamma[:, :, None, None]
    tri = jnp.exp(jnp.maximum(diff, 0.0)[None, None] * lg)
    intra_f = jnp.where((diff >= 0)[None], tri[0], 0.0)
    intra_b = jnp.where((diff > 0)[None], tri[1], 0.0)
    intra_b = intra_b[:, ::-1, ::-1]
    qdec = jnp.exp((pos + 1.0)[None, None, :] * log_gamma[:, :, None])
    kdec = jnp.exp((chunk - 1.0 - pos)[None, None, :] * log_gamma[:, :, None])
    qdec = jnp.stack([qdec[0], qdec[1, :, ::-1]])
    kdec = jnp.stack([kdec[0], kdec[1, :, ::-1]])
    cdec = jnp.exp(chunk * log_gamma)
    rep = lambda a: jnp.broadcast_to(a[..., None], a.shape + (LANES,))
    return (jnp.stack([intra_f, intra_b]), rep(qdec), rep(kdec),
            jnp.broadcast_to(cdec[:, :, None, None], cdec.shape + (8, LANES)))


def _proj_plain_kernel(x_ref, gain_ref, w_ref, o_ref):
    o_ref[...] = _dot(_rms(x_ref[...], gain_ref[...]).astype(BF16), w_ref[...])


def _proj_plain(x, seq_len, gain, w):
    t, d = x.shape
    tm = min(ROW_TILE, seq_len)
    n = w.shape[1]
    return pl.pallas_call(
        _proj_plain_kernel,
        grid=(t // tm,),
        in_specs=[
            pl.BlockSpec((tm, d), lambda i: (i, 0)),
            pl.BlockSpec((1, d), lambda i: (0, 0)),
            pl.BlockSpec((d, n), lambda i: (0, 0)),
        ],
        out_specs=pl.BlockSpec((tm, n), lambda i: (i, 0)),
        out_shape=jax.ShapeDtypeStruct((t, n), F32),
        compiler_params=_cparams(1),
        name="ab_proj_attn",
    )(x, gain, w)


def _mlstm_proj_kernel(x_ref, gain_ref, w_ref, qkv_ref, o_ref, h_ref):
    j = pl.program_id(1)

    @pl.when(j == 0)
    def _():
        h_ref[...] = _rms(x_ref[...], gain_ref[...]).astype(BF16)

    acc = _dot(h_ref[...], w_ref[...])

    @pl.when(j < 3)
    def _():
        scale = jnp.where(j == 1, np.float32(MLSTM_HEAD_DIM ** -0.5), np.float32(1.0))
        qkv_ref[...] = (acc * scale).astype(BF16)

    @pl.when(j == 3)
    def _():
        o_ref[...] = acc


def _mlstm_proj(x, seq_len, gain, w):
    t, d = x.shape
    tm = min(ROW_TILE, seq_len)
    tn = MLSTM_WIDTH
    return pl.pallas_call(
        _mlstm_proj_kernel,
        grid=(t // tm, 4),
        in_specs=[
            pl.BlockSpec((tm, d), lambda i, j: (i, 0)),
            pl.BlockSpec((1, d), lambda i, j: (0, 0)),
            pl.BlockSpec((d, tn), lambda i, j: (0, j)),
        ],
        out_specs=[
            pl.BlockSpec((tm, tn), lambda i, j: (i, jnp.minimum(j, 2))),
            pl.BlockSpec((tm, tn), lambda i, j: (i, 0)),
        ],
        out_shape=[
            jax.ShapeDtypeStruct((t, 3 * MLSTM_WIDTH), BF16),
            jax.ShapeDtypeStruct((t, MLSTM_WIDTH), F32),
        ],
        scratch_shapes=[pltpu.VMEM((tm, d), BF16)],
        compiler_params=_cparams(2),
        name="ab_proj_mlstm",
    )(x, gain, w)


def _attn_prep_kernel(pa_ref, cos_ref, sin_ref, qg_ref, kg_ref, q_ref, kt_ref, v_ref, *, tm):
    lane = lax.broadcasted_iota(jnp.int32, (tm, LANES), 1)
    ri = lax.broadcasted_iota(jnp.int32, (LANES, LANES), 0)
    ci = lax.broadcasted_iota(jnp.int32, (LANES, LANES), 1)
    same_head = ((ri // ATTN_HEAD_DIM) == (ci // ATTN_HEAD_DIM)).astype(BF16)
    c = cos_ref[...]
    s = sin_ref[...]
    first_half = (lane % ATTN_HEAD_DIM) < (ATTN_HEAD_DIM // 2)
    low = lane < ATTN_HEAD_DIM

    def norm_rope(x, gain):
        x2 = x * x
        hi = x2.astype(BF16)
        lo = (x2 - hi.astype(F32)).astype(BF16)
        ssum = _dot(hi, same_head) + _dot(lo, same_head)
        xn = x * lax.rsqrt(ssum * np.float32(1.0 / ATTN_HEAD_DIM) + NORM_EPS) * gain
        swapped = jnp.where(first_half, pltpu.roll(xn, LANES - ATTN_HEAD_DIM // 2, 1),
                            pltpu.roll(xn, ATTN_HEAD_DIM // 2, 1))
        return xn * c + swapped * s

    for p in range(ATTN_HEADS // 2):
        y = norm_rope(pa_ref[:, p * LANES:(p + 1) * LANES], qg_ref[...]) * np.float32(ATTN_HEAD_DIM ** -0.5 * np.log2(np.e))
        yr = pltpu.roll(y, ATTN_HEAD_DIM, 1)
        if (2 * p) // (ATTN_HEADS // ATTN_KV_HEADS) == 0:
            head_a, head_b = jnp.where(low, y, 0.0), jnp.where(low, yr, 0.0)
        else:
            head_a, head_b = jnp.where(low, 0.0, yr), jnp.where(low, 0.0, y)
        q_ref[0, 2 * p] = head_a.astype(BF16)
        q_ref[0, 2 * p + 1] = head_b.astype(BF16)
    k = norm_rope(pa_ref[:, ATTN_Q_WIDTH:ATTN_Q_WIDTH + LANES], kg_ref[...])
    kt_ref[0] = k.T.astype(BF16)
    v_ref[0] = pa_ref[:, ATTN_Q_WIDTH + LANES:ATTN_Q_WIDTH + 2 * LANES].astype(BF16)


def _attn_prep(pa, batch, seq_len, cos, sin, qg, kg):
    tm = min(ROW_TILE, seq_len)
    sblk = seq_len // tm
    n = pa.shape[1]
    return pl.pallas_call(
        functools.partial(_attn_prep_kernel, tm=tm),
        grid=(batch, sblk),
        in_specs=[
            pl.BlockSpec((tm, n), lambda b, s: (b * sblk + s, 0)),
            pl.BlockSpec((tm, LANES), lambda b, s: (s, 0)),
            pl.BlockSpec((tm, LANES), lambda b, s: (s, 0)),
            pl.BlockSpec((1, LANES), lambda b, s: (0, 0)),
            pl.BlockSpec((1, LANES), lambda b, s: (0, 0)),
        ],
        out_specs=[
            pl.BlockSpec((1, ATTN_HEADS, tm, LANES), lambda b, s: (b, 0, s, 0)),
            pl.BlockSpec((1, LANES, tm), lambda b, s: (b, 0, s)),
            pl.BlockSpec((1, tm, LANES), lambda b, s: (b, s, 0)),
        ],
        out_shape=[
            jax.ShapeDtypeStruct((batch, ATTN_HEADS, seq_len, LANES), BF16),
            jax.ShapeDtypeStruct((batch, LANES, seq_len), BF16),
            jax.ShapeDtypeStruct((batch, seq_len, LANES), BF16),
        ],
        compiler_params=_cparams(2),
        name="attn_prep",
    )(pa, cos, sin, qg, kg)


def _attn_kernel(q_ref, kt_ref, v_ref, o_ref, *, tq, tk, nk):
    rows = ATTN_HEADS * tq
    q = q_ref[0].reshape(rows, LANES)
    m = jnp.full((rows, 1), -jnp.inf, F32)
    l = jnp.zeros((rows, 1), F32)
    acc = jnp.zeros((rows, LANES), F32)
    for j in range(nk):
        s = _dot(q, kt_ref[0, :, j * tk:(j + 1) * tk])
        m_new = jnp.maximum(m, jnp.max(s, axis=-1, keepdims=True))
        alpha = jnp.exp2(m - m_new)
        p = jnp.exp2(s - m_new)
        l = alpha * l + jnp.sum(p, axis=-1, keepdims=True)
        acc = alpha * acc + _dot(p.astype(BF16), v_ref[0, j * tk:(j + 1) * tk, :])
        m = m_new
    o = acc / l
    lane = lax.broadcasted_iota(jnp.int32, (tq, LANES), 1)
    low = lane < ATTN_HEAD_DIM
    for p2 in range(ATTN_HEADS // 2):
        a = o[(2 * p2) * tq:(2 * p2 + 1) * tq]
        b = o[(2 * p2 + 1) * tq:(2 * p2 + 2) * tq]
        if (2 * p2) // (ATTN_HEADS // ATTN_KV_HEADS) == 0:
            pair = jnp.where(low, a, pltpu.roll(b, ATTN_HEAD_DIM, 1))
        else:
            pair = jnp.where(low, pltpu.roll(a, ATTN_HEAD_DIM, 1), b)
        o_ref[0, :, p2 * LANES:(p2 + 1) * LANES] = pair.astype(BF16)


def _attention(q, kt, v, batch, seq_len):
    tq = CHUNK
    tk = min(ATTN_K_TILE, seq_len)
    nk = seq_len // tk
    return pl.pallas_call(
        functools.partial(_attn_kernel, tq=tq, tk=tk, nk=nk),
        grid=(batch, seq_len // tq),
        in_specs=[
            pl.BlockSpec((1, ATTN_HEADS, tq, LANES), lambda b, i: (b, 0, i, 0)),
            pl.BlockSpec((1, LANES, seq_len), lambda b, i: (b, 0, 0)),
            pl.BlockSpec((1, seq_len, LANES), lambda b, i: (b, 0, 0)),
        ],
        out_specs=pl.BlockSpec((1, tq, ATTN_Q_WIDTH), lambda b, i: (b, i, 0)),
        out_shape=jax.ShapeDtypeStruct((batch, seq_len, ATTN_Q_WIDTH), BF16),
        compiler_params=_cparams(2),
        name="attention",
    )(q, kt, v)


def _mlstm_kernel(qkvf_ref, gf_ref, qkvb_ref, gb_ref, bias_ref, hf_ref, hb_ref, c_ref, n_ref, m_ref, *, bb):
    c = pl.program_id(1)

    @pl.when(c == 0)
    def _():
        c_ref[...] = jnp.zeros_like(c_ref)
        n_ref[...] = jnp.zeros_like(n_ref)
        m_ref[...] = jnp.zeros_like(m_ref)

    L = CHUNK
    dh = MLSTM_HEAD_DIM
    row = lax.broadcasted_iota(jnp.int32, (L, L), 0)
    col = lax.broadcasted_iota(jnp.int32, (L, L), 1)
    def chain(bi, d, qkv_ref, g_ref, h_ref):
        allowed = (col <= row) if d == 0 else (col >= row)
        gates = g_ref[bi] + bias_ref[...]
        gates_t = gates.T
        logf = _log_sigmoid(gates)
        logf_t = _log_sigmoid(gates_t[0:MLSTM_GATES, :])
        sel_col = allowed.astype(BF16)
        sel_row = ((row <= col) if d == 0 else (row >= col)).astype(BF16)
        f1, f2, f3 = _split3(logf)
        bcum = _dot(sel_col, f1) + _dot(sel_col, f2) + _dot(sel_col, f3)
        t1, t2, t3 = _split3(logf_t)
        bcum_t = _dot(t1, sel_row) + _dot(t2, sel_row) + _dot(t3, sel_row)
        for hd in range(MLSTM_HEADS):
            idx = (bi * 2 + d) * MLSTM_HEADS + hd
            ig = 2 * MLSTM_HEADS * d + hd
            fg = ig + MLSTM_HEADS
            q = qkv_ref[bi, :, hd * dh:(hd + 1) * dh]
            k = qkv_ref[bi, :, MLSTM_WIDTH + hd * dh:MLSTM_WIDTH + (hd + 1) * dh]
            v = qkv_ref[bi, :, 2 * MLSTM_WIDTH + hd * dh:2 * MLSTM_WIDTH + (hd + 1) * dh]
            i_row = gates_t[ig:ig + 1, :]
            f_row = logf_t[fg:fg + 1, :]
            b_row = bcum_t[fg:fg + 1, :]
            i_col = jnp.sum(jnp.where(col == ig, gates, 0.0), axis=-1, keepdims=True)
            b_col = jnp.sum(jnp.where(col == fg, bcum, 0.0), axis=-1, keepdims=True)
            m_old = m_ref[idx][:, 0:1]
            n_old = n_ref[idx]
            c_old = c_ref[idx]
            a_row = i_row - b_row
            dlog = jnp.where(allowed, b_col + a_row, -jnp.inf)
            inter = b_col + m_old
            m_row = jnp.maximum(inter, jnp.max(dlog, axis=-1, keepdims=True))
            w = jnp.exp(dlog - m_row) * _dot_nt(q, k)
            inter_w = jnp.exp(inter - m_row)
            num = _dot(w.astype(BF16), v) + inter_w * _dot(q, c_old.astype(BF16))
            qn = jnp.sum(q.astype(F32) * n_old, axis=-1, keepdims=True)
            den = jnp.sum(w, axis=-1, keepdims=True) + inter_w * qn
            h_ref[bi, :, hd * dh:(hd + 1) * dh] = num / jnp.maximum(jnp.abs(den), jnp.exp(-m_row))
            b_last = jnp.sum(f_row, axis=-1, keepdims=True)
            m_new = jnp.maximum(b_last + m_old, jnp.max(b_last + a_row, axis=-1, keepdims=True))
            keep = jnp.exp(b_last + m_old - m_new)
            kw = k.astype(F32) * jnp.exp(b_last + (i_col - b_col) - m_new)
            c_ref[idx] = keep * c_old + _dot_tn(kw.astype(BF16), v)
            n_ref[idx] = keep * n_old + jnp.sum(kw, axis=0, keepdims=True)
            m_ref[idx] = jnp.broadcast_to(m_new, (1, LANES))

    for bi in range(bb):
        chain(bi, 0, qkvf_ref, gf_ref, hf_ref)
        chain(bi, 1, qkvb_ref, gb_ref, hb_ref)


def _mlstm(qkv, pa, bias, batch, seq_len):
    nc = seq_len // CHUNK
    bb = min(MLSTM_BATCH_BLOCK, batch)
    qkv = qkv.reshape(batch, seq_len, qkv.shape[-1])
    pa = pa.reshape(batch, seq_len, pa.shape[-1])
    gate_blk = (ATTN_Q_WIDTH + 2 * ATTN_KV_WIDTH) // LANES
    fwd = lambda b, c: (b, c, 0)
    bwd = lambda b, c: (b, nc - 1 - c, 0)
    nstate = bb * 2 * MLSTM_HEADS
    hf, hb = pl.pallas_call(
        functools.partial(_mlstm_kernel, bb=bb),
        grid=(batch // bb, nc),
        in_specs=[
            pl.BlockSpec((bb, CHUNK, 3 * MLSTM_WIDTH), fwd),
            pl.BlockSpec((bb, CHUNK, LANES), lambda b, c: (b, c, gate_blk)),
            pl.BlockSpec((bb, CHUNK, 3 * MLSTM_WIDTH), bwd),
            pl.BlockSpec((bb, CHUNK, LANES), lambda b, c: (b, nc - 1 - c, gate_blk)),
            pl.BlockSpec((1, LANES), lambda b, c: (0, 0)),
        ],
        out_specs=[
            pl.BlockSpec((bb, CHUNK, MLSTM_WIDTH), fwd),
            pl.BlockSpec((bb, CHUNK, MLSTM_WIDTH), bwd),
        ],
        out_shape=[jax.ShapeDtypeStruct((batch, seq_len, MLSTM_WIDTH), F32)] * 2,
        scratch_shapes=[
            pltpu.VMEM((nstate, MLSTM_HEAD_DIM, MLSTM_HEAD_DIM), F32),
            pltpu.VMEM((nstate, 1, MLSTM_HEAD_DIM), F32),
            pltpu.VMEM((nstate, 1, LANES), F32),
        ],
        compiler_params=_cparams(2),
        name="mlstm",
    )(qkv, pa, qkv, pa, bias)
    return hf.reshape(batch * seq_len, MLSTM_WIDTH), hb.reshape(batch * seq_len, MLSTM_WIDTH)


def _ab_out_kernel(a_ref, hf_ref, hb_ref, mo_ref, gain_ref, w_ref, x_ref, o_ref):
    acc = x_ref[...] + _dot(a_ref[...], w_ref[0:ATTN_Q_WIDTH, :])
    for hd in range(MLSTM_HEADS):
        sl = slice(hd * MLSTM_HEAD_DIM, (hd + 1) * MLSTM_HEAD_DIM)
        y = _rms(hf_ref[:, sl] + hb_ref[:, sl], gain_ref[:, sl])
        z = (y * jax.nn.sigmoid(mo_ref[:, sl])).astype(BF16)
        acc = acc + _dot(z, w_ref[ATTN_Q_WIDTH + hd * MLSTM_HEAD_DIM:ATTN_Q_WIDTH + (hd + 1) * MLSTM_HEAD_DIM, :])
    o_ref[...] = acc


def _ab_out(attn, hf, hb, mo, gain, w, x, seq_len):
    t, d = x.shape
    tm = min(ROW_TILE, seq_len)
    row = lambda i: (i, 0)
    fixed = lambda i: (0, 0)
    return pl.pallas_call(
        _ab_out_kernel,
        grid=(t // tm,),
        in_specs=[
            pl.BlockSpec((tm, ATTN_Q_WIDTH), row),
            pl.BlockSpec((tm, MLSTM_WIDTH), row),
            pl.BlockSpec((tm, MLSTM_WIDTH), row),
            pl.BlockSpec((tm, MLSTM_WIDTH), row),
            pl.BlockSpec((1, MLSTM_WIDTH), fixed),
            pl.BlockSpec(w.shape, fixed),
            pl.BlockSpec((tm, d), row),
        ],
        out_specs=pl.BlockSpec((tm, d), row),
        out_shape=jax.ShapeDtypeStruct((t, d), F32),
        compiler_params=_cparams(1),
        name="ab_out",
    )(attn, hf, hb, mo, gain, w, x)


def _rope_angles(seq_len, head_dim):
    rows = seq_len // GRID_W
    row_idx = jnp.repeat(jnp.arange(rows, dtype=F32), GRID_W)
    col_idx = jnp.tile(jnp.arange(GRID_W, dtype=F32), rows)
    axis_dim = head_dim // 2
    inv_freq = ROPE_THETA ** (-jnp.arange(0, axis_dim, 2, dtype=F32) / axis_dim)
    ang = jnp.concatenate([row_idx[:, None] * inv_freq, col_idx[:, None] * inv_freq], axis=-1)
    return jnp.cos(ang), jnp.sin(ang)


def _prepare(p):
    d = p["ab_w_in"].shape[1]
    n_ab = p["ab_w_in"].shape[0]
    a_cols = ATTN_Q_WIDTH + 2 * ATTN_KV_WIDTH
    m_lo = a_cols
    m_hi = a_cols + 4 * MLSTM_WIDTH
    w_in = p["ab_w_in"]
    pad = jnp.zeros((n_ab, d, LANES - MLSTM_GATES), w_in.dtype)
    out = dict(
        ab_w_attn=jnp.concatenate([w_in[:, :, :a_cols], w_in[:, :, m_hi:], pad], axis=-1).astype(BF16),
        ab_w_mlstm=w_in[:, :, m_lo:m_hi].astype(BF16),
        ab_bias=jnp.pad(p["ab_gate_bias"].astype(F32), ((0, 0), (0, LANES - MLSTM_GATES)))[:, None, :],
        ab_w_out=p["ab_w_out"].astype(BF16),
        attn_qg=jnp.tile(p["attn_q_norm"].astype(F32), (1, LANES // ATTN_HEAD_DIM))[:, None, :],
        attn_kg=jnp.tile(p["attn_k_norm"].astype(F32), (1, LANES // ATTN_HEAD_DIM))[:, None, :],
        mlstm_gain=p["mlstm_out_norm"].astype(F32)[:, None, :],
        ret_w_in=p["ret_w_in"].astype(BF16),
        ret_w_out=p["ret_w_out"].astype(BF16),
        ret_gain=p["ret_out_norm"].astype(F32)[:, None, :],
        ret_decay_logit=p["ret_decay_logit"],
        norm_mix=p["norm_mix"].astype(F32)[:, None, :],
        norm_ffn=p["norm_ffn"].astype(F32)[:, None, :],
        norm_final=p["norm_final"].astype(F32)[None, :],
    )
    depth, _, ff2 = p["ffn_w_up"].shape
    dff = ff2 // 2
    nf = dff // FFN_COL_TILE
    up = p["ffn_w_up"].astype(BF16)
    tiles = lambda w: w.reshape(depth, d, nf, FFN_COL_TILE).transpose(0, 2, 1, 3)
    out["ffn_wu"] = tiles(up[:, :, :dff])
    out["ffn_wg"] = tiles(up[:, :, dff:])
    out["ffn_wd"] = p["ffn_w_down"].astype(BF16).reshape(depth, nf, FFN_COL_TILE, d)
    cw = p["ffn_conv_w"].astype(F32).reshape(depth, 3, nf, FFN_COL_TILE).transpose(0, 2, 1, 3)
    out["ffn_cw"] = jnp.pad(cw, ((0, 0), (0, 0), (0, 5), (0, 0)))
    out["ffn_cb"] = p["ffn_conv_b"].astype(F32).reshape(depth, nf, 1, FFN_COL_TILE)
    return out


def _trunk(x3, w):
    batch, seq_len, d = x3.shape
    x = x3.reshape(batch * seq_len, d)
    cos_a, sin_a = _rope_angles(seq_len, ATTN_HEAD_DIM)
    reps = LANES // (ATTN_HEAD_DIM // 2)
    cos_a = jnp.tile(cos_a, (1, reps))
    sin_a = jnp.tile(jnp.concatenate([-sin_a, sin_a], axis=-1), (1, reps // 2))
    cos_r, sin_r = _rope_angles(seq_len, RET_QK_DIM)
    depth = w["norm_mix"].shape[0]
    for layer in range(depth):
        j = layer // 2
        if layer % 2 == 0:
            pa = _proj_plain(x, seq_len, w["norm_mix"][layer], w["ab_w_attn"][j])
            qkv, mo = _mlstm_proj(x, seq_len, w["norm_mix"][layer], w["ab_w_mlstm"][j])
            q, kt, v = _attn_prep(pa, batch, seq_len, cos_a, sin_a, w["attn_qg"][j], w["attn_kg"][j])
            attn = _attention(q, kt, v, batch, seq_len).reshape(batch * seq_len, ATTN_Q_WIDTH)
            hf, hb = _mlstm(qkv, pa, w["ab_bias"][j], batch, seq_len)
            x = _ab_out(attn, hf, hb, mo, w["mlstm_gain"][j], w["ab_w_out"][j], x, seq_len)
        else:
            qk, v, g = _ret_proj(x, seq_len, w["norm_mix"][layer], w["ret_w_in"][j], cos_r, sin_r)
            tables = _ret_tables(w["ret_decay_logit"][j], min(RET_CHUNK, seq_len))
            yf, yb = _ret_core(qk, v, batch, seq_len, *tables)
            x = _ret_out(yf, yb, g, w["ret_gain"][j], w["ret_w_out"][j], x, seq_len)
        x = _ffn(x, seq_len, w["norm_ffn"][layer], w["ffn_wu"][layer], w["ffn_wg"][layer], w["ffn_cw"][layer],
                 w["ffn_cb"][layer], w["ffn_wd"][layer], w["norm_final"], final=(layer == depth - 1))
    return x.reshape(batch, seq_len, d)


def kernel(x_prompt, x_sample, norm_mix, norm_ffn, norm_final, ab_w_in, ab_gate_bias, attn_q_norm, attn_k_norm,
           mlstm_out_norm, ab_w_out, ret_w_in, ret_decay_logit, ret_out_norm, ret_w_out, ffn_w_up, ffn_conv_w,
           ffn_conv_b, ffn_w_down):
    w = _prepare(dict(
        norm_mix=norm_mix, norm_ffn=norm_ffn, norm_final=norm_final, ab_w_in=ab_w_in, ab_gate_bias=ab_gate_bias,
        attn_q_norm=attn_q_norm, attn_k_norm=attn_k_norm, mlstm_out_norm=mlstm_out_norm, ab_w_out=ab_w_out,
        ret_w_in=ret_w_in, ret_decay_logit=ret_decay_logit, ret_out_norm=ret_out_norm, ret_w_out=ret_w_out,
        ffn_w_up=ffn_w_up, ffn_conv_w=ffn_conv_w, ffn_conv_b=ffn_conv_b, ffn_w_down=ffn_w_down))
    return (_trunk(x_prompt, w), _trunk(x_sample, w))
```

```python
import functools

import numpy as np
import jax
import jax.numpy as jnp
from jax import lax
from jax.experimental import pallas as pl
from jax.experimental.pallas import tpu as pltpu

F32 = jnp.float32
BF16 = jnp.bfloat16

GRID_W = 64
ROPE_THETA = 10000.0
NORM_EPS = 1e-6
ATTN_HEADS = 8
ATTN_KV_HEADS = 2
ATTN_HEAD_DIM = 64
ATTN_Q_WIDTH = ATTN_HEADS * ATTN_HEAD_DIM
ATTN_KV_WIDTH = ATTN_KV_HEADS * ATTN_HEAD_DIM
MLSTM_HEADS = 4
MLSTM_HEAD_DIM = 128
MLSTM_WIDTH = MLSTM_HEADS * MLSTM_HEAD_DIM
MLSTM_GATES = 4 * MLSTM_HEADS
RET_HEADS = 4
RET_QK_DIM = 256
RET_V_DIM = 512
RET_QK_WIDTH = RET_HEADS * RET_QK_DIM
RET_V_WIDTH = RET_HEADS * RET_V_DIM
CHUNK = 128
RET_CHUNK = 256

LANES = 128
BF16_SUBLANES = 16
VMEM_LIMIT_BYTES = 56 * 1024 * 1024

ROW_TILE = 512
FFN_COL_TILE = 256
ATTN_K_TILE = 1024
MLSTM_BATCH_BLOCK = 4


def _cparams(n_axes):
    return pltpu.CompilerParams(
        dimension_semantics=("arbitrary",) * n_axes, vmem_limit_bytes=VMEM_LIMIT_BYTES)


def _rms(x, gain):
    ms = jnp.mean(x * x, axis=-1, keepdims=True)
    return x * lax.rsqrt(ms + NORM_EPS) * gain


def _dot(a, b):
    return jnp.dot(a, b, preferred_element_type=F32)


def _dot_nt(a, b):
    return lax.dot_general(a, b, (((1,), (1,)), ((), ())), preferred_element_type=F32)


def _dot_tn(a, b):
    return lax.dot_general(a, b, (((0,), (0,)), ((), ())), preferred_element_type=F32)


def _split3(x):
    x1 = x.astype(BF16)
    r1 = x - x1.astype(F32)
    x2 = r1.astype(BF16)
    x3 = (r1 - x2.astype(F32)).astype(BF16)
    return x1, x2, x3


def _log_sigmoid(x):
    return jnp.minimum(x, 0.0) - jnp.log1p(jnp.exp(-jnp.abs(x)))


def _ffn_kernel(xp_ref, x_ref, xn_ref, gain_ref, wu_ref, wg_ref, cw_ref, cb_ref, wd_ref, fin_ref,
                o_ref, h_ref, g_ref, a_ref, *, tm, nf, tf, blocks_per_seq, final):
    i = pl.program_id(0)
    halo = BF16_SUBLANES
    pos = i % blocks_per_seq
    gain = gain_ref[...]
    hp = jnp.where(pos == 0, 0.0, _rms(xp_ref[...], gain))
    hn = jnp.where(pos == blocks_per_seq - 1, 0.0, _rms(xn_ref[...], gain))
    x = x_ref[...]
    h_ref[0:halo, :] = hp.astype(BF16)
    h_ref[halo:halo + tm, :] = _rms(x, gain).astype(BF16)
    h_ref[halo + tm:, :] = hn.astype(BF16)
    sqrt_half = np.float32(np.sqrt(0.5))
    for j in range(nf):
        g_ref[j] = _dot(h_ref[...], wg_ref[j])
        cw = cw_ref[j]
        gc = (g_ref[j, halo - 1:halo - 1 + tm, :] * cw[0:1, :] + g_ref[j, halo:halo + tm, :] * cw[1:2, :]
              + g_ref[j, halo + 1:halo + 1 + tm, :] * cw[2:3, :] + cb_ref[j])
        u = _dot(h_ref[halo:halo + tm, :], wu_ref[j])
        act = 0.5 * gc * (1.0 + lax.erf(gc * sqrt_half))
        a_ref[:, j * tf:(j + 1) * tf] = (act * u).astype(BF16)
    y = x + _dot(a_ref[...], wd_ref[...])
    o_ref[...] = _rms(y, fin_ref[...]) if final else y


def _ffn(x, seq_len, gain, wu, wg, cw, cb, wd, fin_gain, final):
    t, d = x.shape
    tm = min(ROW_TILE, seq_len)
    nf, _, tf = wu.shape
    halo = BF16_SUBLANES
    nblk = t // tm
    hb = tm // halo
    const3 = lambda i: (0, 0, 0)
    kern = functools.partial(_ffn_kernel, tm=tm, nf=nf, tf=tf, blocks_per_seq=seq_len // tm, final=final)
    return pl.pallas_call(
        kern,
        grid=(nblk,),
        in_specs=[
            pl.BlockSpec((halo, d), lambda i: (jnp.maximum(i * hb - 1, 0), 0)),
            pl.BlockSpec((tm, d), lambda i: (i, 0)),
            pl.BlockSpec((halo, d), lambda i: (jnp.minimum((i + 1) * hb, nblk * hb - 1), 0)),
            pl.BlockSpec((1, d), lambda i: (0, 0)),
            pl.BlockSpec(wu.shape, const3, pipeline_mode=pl.Buffered(1)),
            pl.BlockSpec(wg.shape, const3, pipeline_mode=pl.Buffered(1)),
            pl.BlockSpec(cw.shape, const3),
            pl.BlockSpec(cb.shape, const3),
            pl.BlockSpec(wd.shape, lambda i: (0, 0), pipeline_mode=pl.Buffered(1)),
            pl.BlockSpec((1, d), lambda i: (0, 0)),
        ],
        out_specs=pl.BlockSpec((tm, d), lambda i: (i, 0)),
        out_shape=jax.ShapeDtypeStruct((t, d), F32),
        scratch_shapes=[
            pltpu.VMEM((tm + 2 * halo, d), BF16),
            pltpu.VMEM((nf, tm + 2 * halo, tf), F32),
            pltpu.VMEM((tm, nf * tf), BF16),
        ],
        compiler_params=_cparams(1),
        name="conv_ffn",
    )(x, x, x, gain, wu, wg, cw, cb, wd, fin_gain)


def _ret_proj_kernel(x_ref, gain_ref, w_ref, cos_ref, sin_ref, qk_ref, v_ref, g_ref):
    h = _rms(x_ref[...], gain_ref[...]).astype(BF16)
    c = cos_ref[...]
    s = sin_ref[...]
    half = RET_QK_DIM // 2
    for which, scale in ((0, 1.0), (1, RET_QK_DIM ** -0.5)):
        for hd in range(RET_HEADS):
            lo = which * RET_QK_WIDTH + hd * RET_QK_DIM
            acc = _dot(h, w_ref[:, lo:lo + RET_QK_DIM])
            x1 = acc[:, :half]
            x2 = acc[:, half:]
            r1 = x1 * c - x2 * s
            r2 = x2 * c + x1 * s
            if which == 1:
                r1 = r1 * np.float32(scale)
                r2 = r2 * np.float32(scale)
            qk_ref[:, lo:lo + half] = r1.astype(BF16)
            qk_ref[:, lo + half:lo + 2 * half] = r2.astype(BF16)
    v0 = 2 * RET_QK_WIDTH
    g0 = v0 + RET_V_WIDTH
    for hd in range(RET_HEADS):
        sl = slice(hd * RET_V_DIM, (hd + 1) * RET_V_DIM)
        v_ref[:, sl] = _dot(h, w_ref[:, v0 + hd * RET_V_DIM:v0 + (hd + 1) * RET_V_DIM]).astype(BF16)
        g_ref[:, sl] = _dot(h, w_ref[:, g0 + hd * RET_V_DIM:g0 + (hd + 1) * RET_V_DIM])


def _ret_proj(x, seq_len, gain, w, cos, sin):
    t, d = x.shape
    tm = min(ROW_TILE, seq_len)
    sblk = seq_len // tm
    row = lambda i: (i, 0)
    return pl.pallas_call(
        _ret_proj_kernel,
        grid=(t // tm,),
        in_specs=[
            pl.BlockSpec((tm, d), row),
            pl.BlockSpec((1, d), lambda i: (0, 0)),
            pl.BlockSpec(w.shape, lambda i: (0, 0), pipeline_mode=pl.Buffered(1)),
            pl.BlockSpec((tm, LANES), lambda i: (i % sblk, 0)),
            pl.BlockSpec((tm, LANES), lambda i: (i % sblk, 0)),
        ],
        out_specs=[
            pl.BlockSpec((tm, 2 * RET_QK_WIDTH), row),
            pl.BlockSpec((tm, RET_V_WIDTH), row),
            pl.BlockSpec((tm, RET_V_WIDTH), row),
        ],
        out_shape=[
            jax.ShapeDtypeStruct((t, 2 * RET_QK_WIDTH), BF16),
            jax.ShapeDtypeStruct((t, RET_V_WIDTH), BF16),
            jax.ShapeDtypeStruct((t, RET_V_WIDTH), F32),
        ],
        compiler_params=_cparams(1),
        name="ret_proj",
    )(x, gain, w, cos, sin)


def _ret_kernel(qkf_ref, vf_ref, qkb_ref, vb_ref, intra_ref, qd_ref, kd_ref, cd_ref,
                yf_ref, yb_ref, s_ref):
    c = pl.program_id(1)

    @pl.when(c == 0)
    def _():
        s_ref[...] = jnp.zeros_like(s_ref)

    for d, (qk_ref, v_ref, y_ref) in enumerate(((qkf_ref, vf_ref, yf_ref), (qkb_ref, vb_ref, yb_ref))):
        for hd in range(RET_HEADS):
            idx = d * RET_HEADS + hd
            q = qk_ref[:, hd * RET_QK_DIM:(hd + 1) * RET_QK_DIM]
            k = qk_ref[:, RET_QK_WIDTH + hd * RET_QK_DIM:RET_QK_WIDTH + (hd + 1) * RET_QK_DIM]
            v = v_ref[:, hd * RET_V_DIM:(hd + 1) * RET_V_DIM]
            state = s_ref[idx]
            scores = _dot_nt(q, k) * intra_ref[d, hd]
            inter = _dot(q, state.astype(BF16)) * qd_ref[d, hd][:, 0:1]
            y_ref[:, hd * RET_V_DIM:(hd + 1) * RET_V_DIM] = _dot(scores.astype(BF16), v) + inter
            kdec = (k.astype(F32) * kd_ref[d, hd][:, 0:1]).astype(BF16)
            s_ref[idx] = cd_ref[d, hd][0:1, 0:1] * state + _dot_tn(kdec, v)


def _ret_core(qk, v, batch, seq_len, intra, qd, kd, cd):
    t = qk.shape[0]
    chunk = intra.shape[-1]
    nc = seq_len // chunk
    fwd = lambda b, c: (b * nc + c, 0)
    bwd = lambda b, c: (b * nc + nc - 1 - c, 0)
    const4 = lambda b, c: (0, 0, 0, 0)
    return pl.pallas_call(
        _ret_kernel,
        grid=(batch, nc),
        in_specs=[
            pl.BlockSpec((chunk, 2 * RET_QK_WIDTH), fwd),
            pl.BlockSpec((chunk, RET_V_WIDTH), fwd),
            pl.BlockSpec((chunk, 2 * RET_QK_WIDTH), bwd),
            pl.BlockSpec((chunk, RET_V_WIDTH), bwd),
            pl.BlockSpec(intra.shape, const4),
            pl.BlockSpec(qd.shape, const4),
            pl.BlockSpec(kd.shape, const4),
            pl.BlockSpec(cd.shape, const4),
        ],
        out_specs=[
            pl.BlockSpec((chunk, RET_V_WIDTH), fwd),
            pl.BlockSpec((chunk, RET_V_WIDTH), bwd),
        ],
        out_shape=[jax.ShapeDtypeStruct((t, RET_V_WIDTH), F32)] * 2,
        scratch_shapes=[pltpu.VMEM((2 * RET_HEADS, RET_QK_DIM, RET_V_DIM), F32)],
        compiler_params=_cparams(2),
        name="retention",
    )(qk, v, qk, v, intra, qd, kd, cd)


def _ret_out_kernel(yf_ref, yb_ref, g_ref, gain_ref, w_ref, x_ref, o_ref):
    acc = x_ref[...]
    for hd in range(RET_HEADS):
        sl = slice(hd * RET_V_DIM, (hd + 1) * RET_V_DIM)
        y = _rms(yf_ref[:, sl] + yb_ref[:, sl], gain_ref[:, sl])
        g = g_ref[:, sl]
        z = (y * (g * jax.nn.sigmoid(g))).astype(BF16)
        acc = acc + _dot(z, w_ref[sl, :])
    o_ref[...] = acc


def _ret_out(yf, yb, g, gain, w, x, seq_len):
    t, d = x.shape
    tm = min(ROW_TILE, seq_len)
    row = lambda i: (i, 0)
    fixed = lambda i: (0, 0)
    return pl.pallas_call(
        _ret_out_kernel,
        grid=(t // tm,),
        in_specs=[
            pl.BlockSpec((tm, RET_V_WIDTH), row),
            pl.BlockSpec((tm, RET_V_WIDTH), row),
            pl.BlockSpec((tm, RET_V_WIDTH), row),
            pl.BlockSpec((1, RET_V_WIDTH), fixed),
            pl.BlockSpec((RET_V_WIDTH, d), fixed),
            pl.BlockSpec((tm, d), row),
        ],
        out_specs=pl.BlockSpec((tm, d), row),
        out_shape=jax.ShapeDtypeStruct((t, d), F32),
        compiler_params=_cparams(1),
        name="ret_out",
    )(yf, yb, g, gain, w, x)


def _ret_tables(decay_logit, chunk):
    log_gamma = jax.nn.log_sigmoid(decay_logit.astype(F32))
    pos = jnp.arange(chunk, dtype=F32)
    diff = pos[:, None] - pos[None, :]
    lg = log_gamma[:, :, None, None]
    tri = jnp.exp(jnp.maximum(diff, 0.0)[None, None] * lg)
    intra_f = jnp.where((diff >= 0)[None], tri[0], 0.0)
    intra_b = jnp.where((diff > 0)[None], tri[1], 0.0)
    intra_b = intra_b[:, ::-1, ::-1]
    qdec = jnp.exp((pos + 1.0)[None, None, :] * log_gamma[:, :, None])
    kdec = jnp.exp((chunk - 1.0 - pos)[None, None, :] * log_gamma[:, :, None])
    qdec = jnp.stack([qdec[0], qdec[1, :, ::-1]])
    kdec = jnp.stack([kdec[0], kdec[1, :, ::-1]])
    cdec = jnp.exp(chunk * log_gamma)
    rep = lambda a: jnp.broadcast_to(a[..., None], a.shape + (LANES,))
    return (jnp.stack([intra_f, intra_b]), rep(qdec), rep(kdec),
            jnp.broadcast_to(cdec[:, :, None, None], cdec.shape + (8, LANES)))


def _proj_plain_kernel(x_ref, gain_ref, w_ref, o_ref, gt_ref):
    acc = _dot(_rms(x_ref[...], gain_ref[...]).astype(BF16), w_ref[...])
    o_ref[...] = acc
    gt_ref[0] = acc[:, acc.shape[1] - LANES:].T[0:MLSTM_GATES, :]


def _proj_plain(x, batch, seq_len, gain, w):
    t, d = x.shape
    tm = min(ROW_TILE, seq_len)
    sblk = seq_len // tm
    n = w.shape[1]
    return pl.pallas_call(
        _proj_plain_kernel,
        grid=(t // tm,),
        in_specs=[
            pl.BlockSpec((tm, d), lambda i: (i, 0)),
            pl.BlockSpec((1, d), lambda i: (0, 0)),
            pl.BlockSpec((d, n), lambda i: (0, 0)),
        ],
        out_specs=[
            pl.BlockSpec((tm, n), lambda i: (i, 0)),
            pl.BlockSpec((1, MLSTM_GATES, tm), lambda i: (i // sblk, 0, i % sblk)),
        ],
        out_shape=[
            jax.ShapeDtypeStruct((t, n), F32),
            jax.ShapeDtypeStruct((batch, MLSTM_GATES, seq_len), F32),
        ],
        compiler_params=_cparams(1),
        name="ab_proj_attn",
    )(x, gain, w)


def _mlstm_proj_kernel(x_ref, gain_ref, w_ref, qv_ref, kt_ref, o_ref):
    h = _rms(x_ref[...], gain_ref[...]).astype(BF16)
    n = MLSTM_WIDTH
    qv_ref[:, 0:n] = _dot(h, w_ref[:, 0:n]).astype(BF16)
    kt_ref[0] = (_dot(h, w_ref[:, n:2 * n]) * np.float32(MLSTM_HEAD_DIM ** -0.5)).T.astype(BF16)
    qv_ref[:, n:2 * n] = _dot(h, w_ref[:, 2 * n:3 * n]).astype(BF16)
    o_ref[...] = _dot(h, w_ref[:, 3 * n:4 * n])


def _mlstm_proj(x, batch, seq_len, gain, w):
    t, d = x.shape
    tm = min(ROW_TILE, seq_len)
    sblk = seq_len // tm
    tn = MLSTM_WIDTH
    return pl.pallas_call(
        _mlstm_proj_kernel,
        grid=(t // tm,),
        in_specs=[
            pl.BlockSpec((tm, d), lambda i: (i, 0)),
            pl.BlockSpec((1, d), lambda i: (0, 0)),
            pl.BlockSpec(w.shape, lambda i: (0, 0), pipeline_mode=pl.Buffered(1)),
        ],
        out_specs=[
            pl.BlockSpec((tm, 2 * tn), lambda i: (i, 0)),
            pl.BlockSpec((1, tn, tm), lambda i: (i // sblk, 0, i % sblk)),
            pl.BlockSpec((tm, tn), lambda i: (i, 0)),
        ],
        out_shape=[
            jax.ShapeDtypeStruct((t, 2 * MLSTM_WIDTH), BF16),
            jax.ShapeDtypeStruct((batch, MLSTM_WIDTH, seq_len), BF16),
            jax.ShapeDtypeStruct((t, MLSTM_WIDTH), F32),
        ],
        compiler_params=_cparams(1),
        name="ab_proj_mlstm",
    )(x, gain, w)


def _attn_prep_kernel(pa_ref, cos_ref, sin_ref, qg_ref, kg_ref, q_ref, kt_ref, v_ref, *, tm):
    lane = lax.broadcasted_iota(jnp.int32, (tm, LANES), 1)
    ri = lax.broadcasted_iota(jnp.int32, (LANES, LANES), 0)
    ci = lax.broadcasted_iota(jnp.int32, (LANES, LANES), 1)
    same_head = ((ri // ATTN_HEAD_DIM) == (ci // ATTN_HEAD_DIM)).astype(BF16)
    c = cos_ref[...]
    s = sin_ref[...]
    first_half = (lane % ATTN_HEAD_DIM) < (ATTN_HEAD_DIM // 2)
    low = lane < ATTN_HEAD_DIM

    def norm_rope(x, gain):
        x2 = x * x
        hi = x2.astype(BF16)
        lo = (x2 - hi.astype(F32)).astype(BF16)
        ssum = _dot(hi, same_head) + _dot(lo, same_head)
        xn = x * lax.rsqrt(ssum * np.float32(1.0 / ATTN_HEAD_DIM) + NORM_EPS) * gain
        swapped = jnp.where(first_half, pltpu.roll(xn, LANES - ATTN_HEAD_DIM // 2, 1),
                            pltpu.roll(xn, ATTN_HEAD_DIM // 2, 1))
        return xn * c + swapped * s

    for p in range(ATTN_HEADS // 2):
        y = norm_rope(pa_ref[:, p * LANES:(p + 1) * LANES], qg_ref[...]) * np.float32(ATTN_HEAD_DIM ** -0.5 * np.log2(np.e))
        yr = pltpu.roll(y, ATTN_HEAD_DIM, 1)
        if (2 * p) // (ATTN_HEADS // ATTN_KV_HEADS) == 0:
            head_a, head_b = jnp.where(low, y, 0.0), jnp.where(low, yr, 0.0)
        else:
            head_a, head_b = jnp.where(low, 0.0, yr), jnp.where(low, 0.0, y)
        q_ref[0, 2 * p] = head_a.astype(BF16)
        q_ref[0, 2 * p + 1] = head_b.astype(BF16)
    k = norm_rope(pa_ref[:, ATTN_Q_WIDTH:ATTN_Q_WIDTH + LANES], kg_ref[...])
    kt_ref[0] = k.T.astype(BF16)
    v_ref[0] = pa_ref[:, ATTN_Q_WIDTH + LANES:ATTN_Q_WIDTH + 2 * LANES].astype(BF16)


def _attn_prep(pa, batch, seq_len, cos, sin, qg, kg):
    tm = min(ROW_TILE, seq_len)
    sblk = seq_len // tm
    n = pa.shape[1]
    return pl.pallas_call(
        functools.partial(_attn_prep_kernel, tm=tm),
        grid=(batch, sblk),
        in_specs=[
            pl.BlockSpec((tm, n), lambda b, s: (b * sblk + s, 0)),
            pl.BlockSpec((tm, LANES), lambda b, s: (s, 0)),
            pl.BlockSpec((tm, LANES), lambda b, s: (s, 0)),
            pl.BlockSpec((1, LANES), lambda b, s: (0, 0)),
            pl.BlockSpec((1, LANES), lambda b, s: (0, 0)),
        ],
        out_specs=[
            pl.BlockSpec((1, ATTN_HEADS, tm, LANES), lambda b, s: (b, 0, s, 0)),
            pl.BlockSpec((1, LANES, tm), lambda b, s: (b, 0, s)),
            pl.BlockSpec((1, tm, LANES), lambda b, s: (b, s, 0)),
        ],
        out_shape=[
            jax.ShapeDtypeStruct((batch, ATTN_HEADS, seq_len, LANES), BF16),
            jax.ShapeDtypeStruct((batch, LANES, seq_len), BF16),
            jax.ShapeDtypeStruct((batch, seq_len, LANES), BF16),
        ],
        compiler_params=_cparams(2),
        name="attn_prep",
    )(pa, cos, sin, qg, kg)


def _attn_kernel(q_ref, kt_ref, v_ref, o_ref, *, tq, tk, nk):
    rows = ATTN_HEADS * tq
    q = q_ref[0].reshape(rows, LANES)
    m = jnp.full((rows, 1), -jnp.inf, F32)
    l = jnp.zeros((rows, 1), F32)
    acc = jnp.zeros((rows, LANES), F32)
    for j in range(nk):
        s = _dot(q, kt_ref[0, :, j * tk:(j + 1) * tk])
        m_new = jnp.maximum(m, jnp.max(s, axis=-1, keepdims=True))
        alpha = jnp.exp2(m - m_new)
        p = jnp.exp2(s - m_new)
        l = alpha * l + jnp.sum(p, axis=-1, keepdims=True)
        acc = alpha * acc + _dot(p.astype(BF16), v_ref[0, j * tk:(j + 1) * tk, :])
        m = m_new
    o = acc / l
    lane = lax.broadcasted_iota(jnp.int32, (tq, LANES), 1)
    low = lane < ATTN_HEAD_DIM
    for p2 in range(ATTN_HEADS // 2):
        a = o[(2 * p2) * tq:(2 * p2 + 1) * tq]
        b = o[(2 * p2 + 1) * tq:(2 * p2 + 2) * tq]
        if (2 * p2) // (ATTN_HEADS // ATTN_KV_HEADS) == 0:
            pair = jnp.where(low, a, pltpu.roll(b, ATTN_HEAD_DIM, 1))
        else:
            pair = jnp.where(low, pltpu.roll(a, ATTN_HEAD_DIM, 1), b)
        o_ref[0, :, p2 * LANES:(p2 + 1) * LANES] = pair.astype(BF16)


def _attention(q, kt, v, batch, seq_len):
    tq = CHUNK
    tk = min(ATTN_K_TILE, seq_len)
    nk = seq_len // tk
    return pl.pallas_call(
        functools.partial(_attn_kernel, tq=tq, tk=tk, nk=nk),
        grid=(batch, seq_len // tq),
        in_specs=[
            pl.BlockSpec((1, ATTN_HEADS, tq, LANES), lambda b, i: (b, 0, i, 0)),
            pl.BlockSpec((1, LANES, seq_len), lambda b, i: (b, 0, 0)),
            pl.BlockSpec((1, seq_len, LANES), lambda b, i: (b, 0, 0)),
        ],
        out_specs=pl.BlockSpec((1, tq, ATTN_Q_WIDTH), lambda b, i: (b, i, 0)),
        out_shape=jax.ShapeDtypeStruct((batch, seq_len, ATTN_Q_WIDTH), BF16),
        compiler_params=_cparams(2),
        name="attention",
    )(q, kt, v)


def _dot3(x, m01, x_is_lhs=True):
    parts = _split3(x)
    if x_is_lhs:
        return _dot(parts[0], m01) + _dot(parts[1], m01) + _dot(parts[2], m01)
    return _dot(m01, parts[0]) + _dot(m01, parts[1]) + _dot(m01, parts[2])


def _mlstm_kernel(qvf_ref, ktf_ref, gf_ref, gtf_ref, qvb_ref, ktb_ref, gb_ref, gtb_ref, bias_ref, bias_t_ref,
                  hf_ref, hb_ref, cn_ref, m_ref, *, bb):
    c = pl.program_id(1)

    @pl.when(c == 0)
    def _():
        cn_ref[...] = jnp.zeros_like(cn_ref)
        m_ref[...] = jnp.zeros_like(m_ref)

    L = CHUNK
    dh = MLSTM_HEAD_DIM
    row = lax.broadcasted_iota(jnp.int32, (L, L), 0)
    col = lax.broadcasted_iota(jnp.int32, (L, L), 1)
    ones = jnp.ones((L, LANES), BF16)

    def chain(bi, d, qv_ref, kt_ref, g_ref, gt_ref, h_ref):
        allowed = (col <= row) if d == 0 else (col >= row)
        gates = g_ref[bi] + bias_ref[...]
        gates_t = gt_ref[bi] + bias_t_ref[...]
        logf = _log_sigmoid(gates)
        logf_t = _log_sigmoid(gates_t)
        sel_col = allowed.astype(BF16)
        sel_row = ((row <= col) if d == 0 else (row >= col)).astype(BF16)
        bcum = _dot3(logf, sel_col, x_is_lhs=False)
        bcum_t = _dot3(logf_t, sel_row)
        btot_t = _dot3(logf_t, ones)
        for hd in range(MLSTM_HEADS):
            idx = (bi * 2 + d) * MLSTM_HEADS + hd
            ig = 2 * MLSTM_HEADS * d + hd
            fg = ig + MLSTM_HEADS
            q = qv_ref[bi, :, hd * dh:(hd + 1) * dh]
            v = qv_ref[bi, :, MLSTM_WIDTH + hd * dh:MLSTM_WIDTH + (hd + 1) * dh]
            kt = kt_ref[bi, hd * dh:(hd + 1) * dh, :]
            v_ones = jnp.concatenate([v, ones], axis=1)
            a_row = gates_t[ig:ig + 1, :] - bcum_t[fg:fg + 1, :]
            b_last = btot_t[fg:fg + 1, :]
            b_col = _dot3(jnp.where(col == fg, bcum, 0.0), ones)
            m_old = m_ref[idx]
            cn_old = cn_ref[idx]
            dlog = jnp.where(allowed, b_col + a_row, -jnp.inf)
            inter = b_col + m_old
            m_row = jnp.maximum(inter, jnp.max(dlog, axis=-1, keepdims=True))
            w = jnp.exp(dlog - m_row) * _dot(q, kt)
            inter_w = jnp.exp(inter - m_row)
            intra = _dot(w.astype(BF16), v_ones)
            carry = _dot(q, cn_old.astype(BF16))
            num = intra[:, :dh] + inter_w * carry[:, :dh]
            den = intra[:, dh:] + inter_w * carry[:, dh:]
            h_ref[bi, :, hd * dh:(hd + 1) * dh] = num / jnp.maximum(jnp.abs(den), jnp.exp(-m_row))
            m_new = jnp.maximum(b_last + m_old, jnp.max(b_last[:, 0:1] + a_row, axis=-1, keepdims=True))
            keep = jnp.exp(b_last + m_old - m_new)
            wk_row = jnp.exp(b_last[:, 0:1] + a_row - m_new[:, 0:1])
            kw_t = (kt.astype(F32) * wk_row).astype(BF16)
            cn_ref[idx] = jnp.concatenate([keep, keep], axis=1) * cn_old + _dot(kw_t, v_ones)
            m_ref[idx] = m_new

    for bi in range(bb):
        chain(bi, 0, qvf_ref, ktf_ref, gf_ref, gtf_ref, hf_ref)
        chain(bi, 1, qvb_ref, ktb_ref, gb_ref, gtb_ref, hb_ref)


def _mlstm(qv, kt, pa, gt, bias, bias_t, batch, seq_len):
    nc = seq_len // CHUNK
    bb = min(MLSTM_BATCH_BLOCK, batch)
    qv = qv.reshape(batch, seq_len, qv.shape[-1])
    pa = pa.reshape(batch, seq_len, pa.shape[-1])
    gate_blk = (ATTN_Q_WIDTH + 2 * ATTN_KV_WIDTH) // LANES
    fwd = lambda b, c: (b, c, 0)
    bwd = lambda b, c: (b, nc - 1 - c, 0)
    fwd_t = lambda b, c: (b, 0, c)
    bwd_t = lambda b, c: (b, 0, nc - 1 - c)
    nstate = bb * 2 * MLSTM_HEADS
    hf, hb = pl.pallas_call(
        functools.partial(_mlstm_kernel, bb=bb),
        grid=(batch // bb, nc),
        in_specs=[
            pl.BlockSpec((bb, CHUNK, 2 * MLSTM_WIDTH), fwd),
            pl.BlockSpec((bb, MLSTM_WIDTH, CHUNK), fwd_t),
            pl.BlockSpec((bb, CHUNK, LANES), lambda b, c: (b, c, gate_blk)),
            pl.BlockSpec((bb, MLSTM_GATES, CHUNK), fwd_t),
            pl.BlockSpec((bb, CHUNK, 2 * MLSTM_WIDTH), bwd),
            pl.BlockSpec((bb, MLSTM_WIDTH, CHUNK), bwd_t),
            pl.BlockSpec((bb, CHUNK, LANES), lambda b, c: (b, nc - 1 - c, gate_blk)),
            pl.BlockSpec((bb, MLSTM_GATES, CHUNK), bwd_t),
            pl.BlockSpec((1, LANES), lambda b, c: (0, 0)),
            pl.BlockSpec((MLSTM_GATES, LANES), lambda b, c: (0, 0)),
        ],
        out_specs=[
            pl.BlockSpec((bb, CHUNK, MLSTM_WIDTH), fwd),
            pl.BlockSpec((bb, CHUNK, MLSTM_WIDTH), bwd),
        ],
        out_shape=[jax.ShapeDtypeStruct((batch, seq_len, MLSTM_WIDTH), F32)] * 2,
        scratch_shapes=[
            pltpu.VMEM((nstate, MLSTM_HEAD_DIM, 2 * MLSTM_HEAD_DIM), F32),
            pltpu.VMEM((nstate, 1, LANES), F32),
        ],
        compiler_params=_cparams(2),
        name="mlstm",
    )(qv, kt, pa, gt, qv, kt, pa, gt, bias, bias_t)
    return hf.reshape(batch * seq_len, MLSTM_WIDTH), hb.reshape(batch * seq_len, MLSTM_WIDTH)


def _ab_out_kernel(a_ref, hf_ref, hb_ref, mo_ref, gain_ref, w_ref, x_ref, o_ref):
    acc = x_ref[...] + _dot(a_ref[...], w_ref[0:ATTN_Q_WIDTH, :])
    for hd in range(MLSTM_HEADS):
        sl = slice(hd * MLSTM_HEAD_DIM, (hd + 1) * MLSTM_HEAD_DIM)
        y = _rms(hf_ref[:, sl] + hb_ref[:, sl], gain_ref[:, sl])
        z = (y * jax.nn.sigmoid(mo_ref[:, sl])).astype(BF16)
        acc = acc + _dot(z, w_ref[ATTN_Q_WIDTH + hd * MLSTM_HEAD_DIM:ATTN_Q_WIDTH + (hd + 1) * MLSTM_HEAD_DIM, :])
    o_ref[...] = acc


def _ab_out(attn, hf, hb, mo, gain, w, x, seq_len):
    t, d = x.shape
    tm = min(ROW_TILE, seq_len)
    row = lambda i: (i, 0)
    fixed = lambda i: (0, 0)
    return pl.pallas_call(
        _ab_out_kernel,
        grid=(t // tm,),
        in_specs=[
            pl.BlockSpec((tm, ATTN_Q_WIDTH), row),
            pl.BlockSpec((tm, MLSTM_WIDTH), row),
            pl.BlockSpec((tm, MLSTM_WIDTH), row),
            pl.BlockSpec((tm, MLSTM_WIDTH), row),
            pl.BlockSpec((1, MLSTM_WIDTH), fixed),
            pl.BlockSpec(w.shape, fixed),
            pl.BlockSpec((tm, d), row),
        ],
        out_specs=pl.BlockSpec((tm, d), row),
        out_shape=jax.ShapeDtypeStruct((t, d), F32),
        compiler_params=_cparams(1),
        name="ab_out",
    )(attn, hf, hb, mo, gain, w, x)


def _rope_angles(seq_len, head_dim):
    rows = seq_len // GRID_W
    row_idx = jnp.repeat(jnp.arange(rows, dtype=F32), GRID_W)
    col_idx = jnp.tile(jnp.arange(GRID_W, dtype=F32), rows)
    axis_dim = head_dim // 2
    inv_freq = ROPE_THETA ** (-jnp.arange(0, axis_dim, 2, dtype=F32) / axis_dim)
    ang = jnp.concatenate([row_idx[:, None] * inv_freq, col_idx[:, None] * inv_freq], axis=-1)
    return jnp.cos(ang), jnp.sin(ang)


def _prepare(p):
    d = p["ab_w_in"].shape[1]
    n_ab = p["ab_w_in"].shape[0]
    a_cols = ATTN_Q_WIDTH + 2 * ATTN_KV_WIDTH
    m_lo = a_cols
    m_hi = a_cols + 4 * MLSTM_WIDTH
    w_in = p["ab_w_in"]
    pad = jnp.zeros((n_ab, d, LANES - MLSTM_GATES), w_in.dtype)
    out = dict(
        ab_w_attn=jnp.concatenate([w_in[:, :, :a_cols], w_in[:, :, m_hi:], pad], axis=-1).astype(BF16),
        ab_w_mlstm=w_in[:, :, m_lo:m_hi].astype(BF16),
        ab_bias=jnp.pad(p["ab_gate_bias"].astype(F32), ((0, 0), (0, LANES - MLSTM_GATES)))[:, None, :],
        ab_bias_t=jnp.broadcast_to(p["ab_gate_bias"].astype(F32)[:, :, None], (n_ab, MLSTM_GATES, LANES)),
        ab_w_out=p["ab_w_out"].astype(BF16),
        attn_qg=jnp.tile(p["attn_q_norm"].astype(F32), (1, LANES // ATTN_HEAD_DIM))[:, None, :],
        attn_kg=jnp.tile(p["attn_k_norm"].astype(F32), (1, LANES // ATTN_HEAD_DIM))[:, None, :],
        mlstm_gain=p["mlstm_out_norm"].astype(F32)[:, None, :],
        ret_w_in=p["ret_w_in"].astype(BF16),
        ret_w_out=p["ret_w_out"].astype(BF16),
        ret_gain=p["ret_out_norm"].astype(F32)[:, None, :],
        ret_decay_logit=p["ret_decay_logit"],
        norm_mix=p["norm_mix"].astype(F32)[:, None, :],
        norm_ffn=p["norm_ffn"].astype(F32)[:, None, :],
        norm_final=p["norm_final"].astype(F32)[None, :],
    )
    depth, _, ff2 = p["ffn_w_up"].shape
    dff = ff2 // 2
    nf = dff // FFN_COL_TILE
    up = p["ffn_w_up"].astype(BF16)
    tiles = lambda w: w.reshape(depth, d, nf, FFN_COL_TILE).transpose(0, 2, 1, 3)
    out["ffn_wu"] = tiles(up[:, :, :dff])
    out["ffn_wg"] = tiles(up[:, :, dff:])
    out["ffn_wd"] = p["ffn_w_down"].astype(BF16)
    cw = p["ffn_conv_w"].astype(F32).reshape(depth, 3, nf, FFN_COL_TILE).transpose(0, 2, 1, 3)
    out["ffn_cw"] = jnp.pad(cw, ((0, 0), (0, 0), (0, 5), (0, 0)))
    out["ffn_cb"] = p["ffn_conv_b"].astype(F32).reshape(depth, nf, 1, FFN_COL_TILE)
    return out


def _trunk(x3, w):
    batch, seq_len, d = x3.shape
    x = x3.reshape(batch * seq_len, d)
    cos_a, sin_a = _rope_angles(seq_len, ATTN_HEAD_DIM)
    reps = LANES // (ATTN_HEAD_DIM // 2)
    cos_a = jnp.tile(cos_a, (1, reps))
    sin_a = jnp.tile(jnp.concatenate([-sin_a, sin_a], axis=-1), (1, reps // 2))
    cos_r, sin_r = _rope_angles(seq_len, RET_QK_DIM)
    depth = w["norm_mix"].shape[0]
    for layer in range(depth):
        j = layer // 2
        if layer % 2 == 0:
            pa, gt = _proj_plain(x, batch, seq_len, w["norm_mix"][layer], w["ab_w_attn"][j])
            qv, mkt, mo = _mlstm_proj(x, batch, seq_len, w["norm_mix"][layer], w["ab_w_mlstm"][j])
            q, kt, v = _attn_prep(pa, batch, seq_len, cos_a, sin_a, w["attn_qg"][j], w["attn_kg"][j])
            attn = _attention(q, kt, v, batch, seq_len).reshape(batch * seq_len, ATTN_Q_WIDTH)
            hf, hb = _mlstm(qv, mkt, pa, gt, w["ab_bias"][j], w["ab_bias_t"][j], batch, seq_len)
            x = _ab_out(attn, hf, hb, mo, w["mlstm_gain"][j], w["ab_w_out"][j], x, seq_len)
        else:
            qk, v, g = _ret_proj(x, seq_len, w["norm_mix"][layer], w["ret_w_in"][j], cos_r, sin_r)
            tables = _ret_tables(w["ret_decay_logit"][j], min(RET_CHUNK, seq_len))
            yf, yb = _ret_core(qk, v, batch, seq_len, *tables)
            x = _ret_out(yf, yb, g, w["ret_gain"][j], w["ret_w_out"][j], x, seq_len)
        x = _ffn(x, seq_len, w["norm_ffn"][layer], w["ffn_wu"][layer], w["ffn_wg"][layer], w["ffn_cw"][layer],
                 w["ffn_cb"][layer], w["ffn_wd"][layer], w["norm_final"], final=(layer == depth - 1))
    return x.reshape(batch, seq_len, d)


def kernel(x_prompt, x_sample, norm_mix, norm_ffn, norm_final, ab_w_in, ab_gate_bias, attn_q_norm, attn_k_norm,
           mlstm_out_norm, ab_w_out, ret_w_in, ret_decay_logit, ret_out_norm, ret_w_out, ffn_w_up, ffn_conv_w,
           ffn_conv_b, ffn_w_down):
    w = _prepare(dict(
        norm_mix=norm_mix, norm_ffn=norm_ffn, norm_final=norm_final, ab_w_in=ab_w_in, ab_gate_bias=ab_gate_bias,
        attn_q_norm=attn_q_norm, attn_k_norm=attn_k_norm, mlstm_out_norm=mlstm_out_norm, ab_w_out=ab_w_out,
        ret_w_in=ret_w_in, ret_decay_logit=ret_decay_logit, ret_out_norm=ret_out_norm, ret_w_out=ret_w_out,
        ffn_w_up=ffn_w_up, ffn_conv_w=ffn_conv_w, ffn_conv_b=ffn_conv_b, ffn_w_down=ffn_w_down))
    return (_trunk(x_prompt, w), _trunk(x_sample, w))
```

```python
import functools

import numpy as np
import jax
import jax.numpy as jnp
from jax import lax
from jax.experimental import pallas as pl
from jax.experimental.pallas import tpu as pltpu

F32 = jnp.float32
BF16 = jnp.bfloat16

GRID_W = 64
ROPE_THETA = 10000.0
NORM_EPS = 1e-6
ATTN_HEADS = 8
ATTN_KV_HEADS = 2
ATTN_HEAD_DIM = 64
ATTN_Q_WIDTH = ATTN_HEADS * ATTN_HEAD_DIM
ATTN_KV_WIDTH = ATTN_KV_HEADS * ATTN_HEAD_DIM
MLSTM_HEADS = 4
MLSTM_HEAD_DIM = 128
MLSTM_WIDTH = MLSTM_HEADS * MLSTM_HEAD_DIM
MLSTM_GATES = 4 * MLSTM_HEADS
RET_HEADS = 4
RET_QK_DIM = 256
RET_V_DIM = 512
RET_QK_WIDTH = RET_HEADS * RET_QK_DIM
RET_V_WIDTH = RET_HEADS * RET_V_DIM
CHUNK = 128
RET_CHUNK = 256

LANES = 128
BF16_SUBLANES = 16
VMEM_LIMIT_BYTES = 56 * 1024 * 1024

ROW_TILE = 512
FFN_COL_TILE = 256
ATTN_K_TILE = 1024
MLSTM_BATCH_BLOCK = 4


def _cparams(n_axes):
    return pltpu.CompilerParams(
        dimension_semantics=("arbitrary",) * n_axes, vmem_limit_bytes=VMEM_LIMIT_BYTES)


def _rms(x, gain):
    ms = jnp.mean(x * x, axis=-1, keepdims=True)
    return x * lax.rsqrt(ms + NORM_EPS) * gain


def _dot(a, b):
    return jnp.dot(a, b, preferred_element_type=F32)


def _dot_nt(a, b):
    return lax.dot_general(a, b, (((1,), (1,)), ((), ())), preferred_element_type=F32)


def _dot_tn(a, b):
    return lax.dot_general(a, b, (((0,), (0,)), ((), ())), preferred_element_type=F32)


def _split3(x):
    x1 = x.astype(BF16)
    r1 = x - x1.astype(F32)
    x2 = r1.astype(BF16)
    x3 = (r1 - x2.astype(F32)).astype(BF16)
    return x1, x2, x3


def _log_sigmoid(x):
    return jnp.minimum(x, 0.0) - jnp.log1p(jnp.exp(-jnp.abs(x)))


def _ffn_kernel(xp_ref, x_ref, xn_ref, gain_ref, wu_ref, wg_ref, cw_ref, cb_ref, wd_ref, fin_ref,
                o_ref, h_ref, g_ref, a_ref, *, tm, nf, tf, blocks_per_seq, final):
    i = pl.program_id(0)
    halo = BF16_SUBLANES
    pos = i % blocks_per_seq
    gain = gain_ref[...]
    hp = jnp.where(pos == 0, 0.0, _rms(xp_ref[...], gain))
    hn = jnp.where(pos == blocks_per_seq - 1, 0.0, _rms(xn_ref[...], gain))
    x = x_ref[...]
    h_ref[0:halo, :] = hp.astype(BF16)
    h_ref[halo:halo + tm, :] = _rms(x, gain).astype(BF16)
    h_ref[halo + tm:, :] = hn.astype(BF16)
    sqrt_half = np.float32(np.sqrt(0.5))
    for j in range(nf):
        g_ref[j] = _dot(h_ref[...], wg_ref[j])
        cw = cw_ref[j]
        gc = (g_ref[j, halo - 1:halo - 1 + tm, :] * cw[0:1, :] + g_ref[j, halo:halo + tm, :] * cw[1:2, :]
              + g_ref[j, halo + 1:halo + 1 + tm, :] * cw[2:3, :] + cb_ref[j])
        u = _dot(h_ref[halo:halo + tm, :], wu_ref[j])
        act = 0.5 * gc * (1.0 + lax.erf(gc * sqrt_half))
        a_ref[:, j * tf:(j + 1) * tf] = (act * u).astype(BF16)
    y = x + _dot(a_ref[...], wd_ref[...])
    o_ref[...] = _rms(y, fin_ref[...]) if final else y


def _ffn(x, seq_len, gain, wu, wg, cw, cb, wd, fin_gain, final):
    t, d = x.shape
    tm = min(ROW_TILE, seq_len)
    nf, _, tf = wu.shape
    halo = BF16_SUBLANES
    nblk = t // tm
    hb = tm // halo
    const3 = lambda i: (0, 0, 0)
    kern = functools.partial(_ffn_kernel, tm=tm, nf=nf, tf=tf, blocks_per_seq=seq_len // tm, final=final)
    return pl.pallas_call(
        kern,
        grid=(nblk,),
        in_specs=[
            pl.BlockSpec((halo, d), lambda i: (jnp.maximum(i * hb - 1, 0), 0)),
            pl.BlockSpec((tm, d), lambda i: (i, 0)),
            pl.BlockSpec((halo, d), lambda i: (jnp.minimum((i + 1) * hb, nblk * hb - 1), 0)),
            pl.BlockSpec((1, d), lambda i: (0, 0)),
            pl.BlockSpec(wu.shape, const3, pipeline_mode=pl.Buffered(1)),
            pl.BlockSpec(wg.shape, const3, pipeline_mode=pl.Buffered(1)),
            pl.BlockSpec(cw.shape, const3),
            pl.BlockSpec(cb.shape, const3),
            pl.BlockSpec(wd.shape, lambda i: (0, 0), pipeline_mode=pl.Buffered(1)),
            pl.BlockSpec((1, d), lambda i: (0, 0)),
        ],
        out_specs=pl.BlockSpec((tm, d), lambda i: (i, 0)),
        out_shape=jax.ShapeDtypeStruct((t, d), F32),
        scratch_shapes=[
            pltpu.VMEM((tm + 2 * halo, d), BF16),
            pltpu.VMEM((nf, tm + 2 * halo, tf), F32),
            pltpu.VMEM((tm, nf * tf), BF16),
        ],
        compiler_params=_cparams(1),
        name="conv_ffn",
    )(x, x, x, gain, wu, wg, cw, cb, wd, fin_gain)


def _ret_proj_kernel(x_ref, gain_ref, w_ref, cos_ref, sin_ref, qk_ref, v_ref, g_ref):
    h = _rms(x_ref[...], gain_ref[...]).astype(BF16)
    c = cos_ref[...]
    s = sin_ref[...]
    half = RET_QK_DIM // 2
    for which, scale in ((0, 1.0), (1, RET_QK_DIM ** -0.5)):
        for hd in range(RET_HEADS):
            lo = which * RET_QK_WIDTH + hd * RET_QK_DIM
            acc = _dot(h, w_ref[:, lo:lo + RET_QK_DIM])
            x1 = acc[:, :half]
            x2 = acc[:, half:]
            r1 = x1 * c - x2 * s
            r2 = x2 * c + x1 * s
            if which == 1:
                r1 = r1 * np.float32(scale)
                r2 = r2 * np.float32(scale)
            qk_ref[:, lo:lo + half] = r1.astype(BF16)
            qk_ref[:, lo + half:lo + 2 * half] = r2.astype(BF16)
    v0 = 2 * RET_QK_WIDTH
    g0 = v0 + RET_V_WIDTH
    for hd in range(RET_HEADS):
        sl = slice(hd * RET_V_DIM, (hd + 1) * RET_V_DIM)
        v_ref[:, sl] = _dot(h, w_ref[:, v0 + hd * RET_V_DIM:v0 + (hd + 1) * RET_V_DIM]).astype(BF16)
        g_ref[:, sl] = _dot(h, w_ref[:, g0 + hd * RET_V_DIM:g0 + (hd + 1) * RET_V_DIM]).astype(BF16)


def _ret_proj(x, seq_len, gain, w, cos, sin):
    t, d = x.shape
    tm = min(ROW_TILE, seq_len)
    sblk = seq_len // tm
    row = lambda i: (i, 0)
    return pl.pallas_call(
        _ret_proj_kernel,
        grid=(t // tm,),
        in_specs=[
            pl.BlockSpec((tm, d), row),
            pl.BlockSpec((1, d), lambda i: (0, 0)),
            pl.BlockSpec(w.shape, lambda i: (0, 0), pipeline_mode=pl.Buffered(1)),
            pl.BlockSpec((tm, LANES), lambda i: (i % sblk, 0)),
            pl.BlockSpec((tm, LANES), lambda i: (i % sblk, 0)),
        ],
        out_specs=[
            pl.BlockSpec((tm, 2 * RET_QK_WIDTH), row),
            pl.BlockSpec((tm, RET_V_WIDTH), row),
            pl.BlockSpec((tm, RET_V_WIDTH), row),
        ],
        out_shape=[
            jax.ShapeDtypeStruct((t, 2 * RET_QK_WIDTH), BF16),
            jax.ShapeDtypeStruct((t, RET_V_WIDTH), BF16),
            jax.ShapeDtypeStruct((t, RET_V_WIDTH), BF16),
        ],
        compiler_params=_cparams(1),
        name="ret_proj",
    )(x, gain, w, cos, sin)


def _ret_kernel(qkf_ref, vf_ref, qkb_ref, vb_ref, intra_ref, qd_ref, kd_ref, cd_ref,
                yf_ref, yb_ref, s_ref):
    c = pl.program_id(1)

    @pl.when(c == 0)
    def _():
        s_ref[...] = jnp.zeros_like(s_ref)

    for d, (qk_ref, v_ref, y_ref) in enumerate(((qkf_ref, vf_ref, yf_ref), (qkb_ref, vb_ref, yb_ref))):
        for hd in range(RET_HEADS):
            idx = d * RET_HEADS + hd
            q = qk_ref[:, hd * RET_QK_DIM:(hd + 1) * RET_QK_DIM]
            k = qk_ref[:, RET_QK_WIDTH + hd * RET_QK_DIM:RET_QK_WIDTH + (hd + 1) * RET_QK_DIM]
            v = v_ref[:, hd * RET_V_DIM:(hd + 1) * RET_V_DIM]
            state = s_ref[idx]
            scores = _dot_nt(q, k) * intra_ref[d, hd]
            inter = _dot(q, state.astype(BF16)) * qd_ref[d, hd][:, 0:1]
            y_ref[:, hd * RET_V_DIM:(hd + 1) * RET_V_DIM] = (_dot(scores.astype(BF16), v) + inter).astype(BF16)
            kdec = (k.astype(F32) * kd_ref[d, hd][:, 0:1]).astype(BF16)
            s_ref[idx] = cd_ref[d, hd][0:1, 0:1] * state + _dot_tn(kdec, v)


def _ret_core(qk, v, batch, seq_len, intra, qd, kd, cd):
    t = qk.shape[0]
    chunk = intra.shape[-1]
    nc = seq_len // chunk
    fwd = lambda b, c: (b * nc + c, 0)
    bwd = lambda b, c: (b * nc + nc - 1 - c, 0)
    const4 = lambda b, c: (0, 0, 0, 0)
    return pl.pallas_call(
        _ret_kernel,
        grid=(batch, nc),
        in_specs=[
            pl.BlockSpec((chunk, 2 * RET_QK_WIDTH), fwd),
            pl.BlockSpec((chunk, RET_V_WIDTH), fwd),
            pl.BlockSpec((chunk, 2 * RET_QK_WIDTH), bwd),
            pl.BlockSpec((chunk, RET_V_WIDTH), bwd),
            pl.BlockSpec(intra.shape, const4),
            pl.BlockSpec(qd.shape, const4),
            pl.BlockSpec(kd.shape, const4),
            pl.BlockSpec(cd.shape, const4),
        ],
        out_specs=[
            pl.BlockSpec((chunk, RET_V_WIDTH), fwd),
            pl.BlockSpec((chunk, RET_V_WIDTH), bwd),
        ],
        out_shape=[jax.ShapeDtypeStruct((t, RET_V_WIDTH), BF16)] * 2,
        scratch_shapes=[pltpu.VMEM((2 * RET_HEADS, RET_QK_DIM, RET_V_DIM), F32)],
        compiler_params=_cparams(2),
        name="retention",
    )(qk, v, qk, v, intra, qd, kd, cd)


def _ret_out_kernel(yf_ref, yb_ref, g_ref, gain_ref, w_ref, x_ref, o_ref):
    acc = x_ref[...]
    for hd in range(RET_HEADS):
        sl = slice(hd * RET_V_DIM, (hd + 1) * RET_V_DIM)
        y = _rms(yf_ref[:, sl].astype(F32) + yb_ref[:, sl].astype(F32), gain_ref[:, sl])
        g = g_ref[:, sl].astype(F32)
        z = (y * (g * jax.nn.sigmoid(g))).astype(BF16)
        acc = acc + _dot(z, w_ref[sl, :])
    o_ref[...] = acc


def _ret_out(yf, yb, g, gain, w, x, seq_len):
    t, d = x.shape
    tm = min(ROW_TILE, seq_len)
    row = lambda i: (i, 0)
    fixed = lambda i: (0, 0)
    return pl.pallas_call(
        _ret_out_kernel,
        grid=(t // tm,),
        in_specs=[
            pl.BlockSpec((tm, RET_V_WIDTH), row),
            pl.BlockSpec((tm, RET_V_WIDTH), row),
            pl.BlockSpec((tm, RET_V_WIDTH), row),
            pl.BlockSpec((1, RET_V_WIDTH), fixed),
            pl.BlockSpec((RET_V_WIDTH, d), fixed),
            pl.BlockSpec((tm, d), row),
        ],
        out_specs=pl.BlockSpec((tm, d), row),
        out_shape=jax.ShapeDtypeStruct((t, d), F32),
        compiler_params=_cparams(1),
        name="ret_out",
    )(yf, yb, g, gain, w, x)


def _ret_tables(decay_logit, chunk):
    log_gamma = jax.nn.log_sigmoid(decay_logit.astype(F32))
    pos = jnp.arange(chunk, dtype=F32)
    diff = pos[:, None] - pos[None, :]
    lg = log_gamma[:, :, None, None]
    tri = jnp.exp(jnp.maximum(diff, 0.0)[None, None] * lg)
    intra_f = jnp.where((diff >= 0)[None], tri[0], 0.0)
    intra_b = jnp.where((diff > 0)[None], tri[1], 0.0)
    intra_b = intra_b[:, ::-1, ::-1]
    qdec = jnp.exp((pos + 1.0)[None, None, :] * log_gamma[:, :, None])
    kdec = jnp.exp((chunk - 1.0 - pos)[None, None, :] * log_gamma[:, :, None])
    qdec = jnp.stack([qdec[0], qdec[1, :, ::-1]])
    kdec = jnp.stack([kdec[0], kdec[1, :, ::-1]])
    cdec = jnp.exp(chunk * log_gamma)
    rep = lambda a: jnp.broadcast_to(a[..., None], a.shape + (LANES,))
    return (jnp.stack([intra_f, intra_b]), rep(qdec), rep(kdec),
            jnp.broadcast_to(cdec[:, :, None, None], cdec.shape + (8, LANES)))


def _mlstm_proj_kernel(x_ref, gain_ref, w_ref, qv_ref, kt_ref, o_ref):
    h = _rms(x_ref[...], gain_ref[...]).astype(BF16)
    n = MLSTM_WIDTH
    qv_ref[:, 0:n] = _dot(h, w_ref[:, 0:n]).astype(BF16)
    kt_ref[0] = (_dot(h, w_ref[:, n:2 * n]) * np.float32(MLSTM_HEAD_DIM ** -0.5)).T.astype(BF16)
    qv_ref[:, n:2 * n] = _dot(h, w_ref[:, 2 * n:3 * n]).astype(BF16)
    o_ref[...] = _dot(h, w_ref[:, 3 * n:4 * n])


def _mlstm_proj(x, batch, seq_len, gain, w):
    t, d = x.shape
    tm = min(ROW_TILE, seq_len)
    sblk = seq_len // tm
    tn = MLSTM_WIDTH
    return pl.pallas_call(
        _mlstm_proj_kernel,
        grid=(t // tm,),
        in_specs=[
            pl.BlockSpec((tm, d), lambda i: (i, 0)),
            pl.BlockSpec((1, d), lambda i: (0, 0)),
            pl.BlockSpec(w.shape, lambda i: (0, 0), pipeline_mode=pl.Buffered(1)),
        ],
        out_specs=[
            pl.BlockSpec((tm, 2 * tn), lambda i: (i, 0)),
            pl.BlockSpec((1, tn, tm), lambda i: (i // sblk, 0, i % sblk)),
            pl.BlockSpec((tm, tn), lambda i: (i, 0)),
        ],
        out_shape=[
            jax.ShapeDtypeStruct((t, 2 * MLSTM_WIDTH), BF16),
            jax.ShapeDtypeStruct((batch, MLSTM_WIDTH, seq_len), BF16),
            jax.ShapeDtypeStruct((t, MLSTM_WIDTH), F32),
        ],
        compiler_params=_cparams(1),
        name="ab_proj_mlstm",
    )(x, gain, w)


def _attn_proj_kernel(x_ref, gain_ref, w_ref, cos_ref, sin_ref, qg_ref, kg_ref,
                      q_ref, kt_ref, v_ref, g_ref, gt_ref, *, tm):
    pa = _dot(_rms(x_ref[...], gain_ref[...]).astype(BF16), w_ref[...])
    gates = pa[:, ATTN_Q_WIDTH + 2 * LANES:]
    g_ref[...] = gates
    gt_ref[0] = gates.T[0:MLSTM_GATES, :]
    lane = lax.broadcasted_iota(jnp.int32, (tm, LANES), 1)
    ri = lax.broadcasted_iota(jnp.int32, (LANES, LANES), 0)
    ci = lax.broadcasted_iota(jnp.int32, (LANES, LANES), 1)
    same_head = ((ri // ATTN_HEAD_DIM) == (ci // ATTN_HEAD_DIM)).astype(BF16)
    c = cos_ref[...]
    s = sin_ref[...]
    first_half = (lane % ATTN_HEAD_DIM) < (ATTN_HEAD_DIM // 2)
    low = lane < ATTN_HEAD_DIM

    def norm_rope(x, gain):
        x2 = x * x
        hi = x2.astype(BF16)
        lo = (x2 - hi.astype(F32)).astype(BF16)
        ssum = _dot(hi, same_head) + _dot(lo, same_head)
        xn = x * lax.rsqrt(ssum * np.float32(1.0 / ATTN_HEAD_DIM) + NORM_EPS) * gain
        swapped = jnp.where(first_half, pltpu.roll(xn, LANES - ATTN_HEAD_DIM // 2, 1),
                            pltpu.roll(xn, ATTN_HEAD_DIM // 2, 1))
        return xn * c + swapped * s

    for p in range(ATTN_HEADS // 2):
        y = norm_rope(pa[:, p * LANES:(p + 1) * LANES], qg_ref[...]) * np.float32(ATTN_HEAD_DIM ** -0.5 * np.log2(np.e))
        yr = pltpu.roll(y, ATTN_HEAD_DIM, 1)
        if (2 * p) // (ATTN_HEADS // ATTN_KV_HEADS) == 0:
            head_a, head_b = jnp.where(low, y, 0.0), jnp.where(low, yr, 0.0)
        else:
            head_a, head_b = jnp.where(low, 0.0, yr), jnp.where(low, 0.0, y)
        q_ref[0, 2 * p] = head_a.astype(BF16)
        q_ref[0, 2 * p + 1] = head_b.astype(BF16)
    k = norm_rope(pa[:, ATTN_Q_WIDTH:ATTN_Q_WIDTH + LANES], kg_ref[...])
    kt_ref[0] = k.T.astype(BF16)
    v_ref[0] = pa[:, ATTN_Q_WIDTH + LANES:ATTN_Q_WIDTH + 2 * LANES].astype(BF16)


def _attn_proj(x, batch, seq_len, gain, w, cos, sin, qg, kg):
    t, d = x.shape
    tm = min(ROW_TILE, seq_len)
    sblk = seq_len // tm
    return pl.pallas_call(
        functools.partial(_attn_proj_kernel, tm=tm),
        grid=(batch, sblk),
        in_specs=[
            pl.BlockSpec((tm, d), lambda b, s: (b * sblk + s, 0)),
            pl.BlockSpec((1, d), lambda b, s: (0, 0)),
            pl.BlockSpec(w.shape, lambda b, s: (0, 0)),
            pl.BlockSpec((tm, LANES), lambda b, s: (s, 0)),
            pl.BlockSpec((tm, LANES), lambda b, s: (s, 0)),
            pl.BlockSpec((1, LANES), lambda b, s: (0, 0)),
            pl.BlockSpec((1, LANES), lambda b, s: (0, 0)),
        ],
        out_specs=[
            pl.BlockSpec((1, ATTN_HEADS, tm, LANES), lambda b, s: (b, 0, s, 0)),
            pl.BlockSpec((1, LANES, tm), lambda b, s: (b, 0, s)),
            pl.BlockSpec((1, tm, LANES), lambda b, s: (b, s, 0)),
            pl.BlockSpec((tm, LANES), lambda b, s: (b * sblk + s, 0)),
            pl.BlockSpec((1, MLSTM_GATES, tm), lambda b, s: (b, 0, s)),
        ],
        out_shape=[
            jax.ShapeDtypeStruct((batch, ATTN_HEADS, seq_len, LANES), BF16),
            jax.ShapeDtypeStruct((batch, LANES, seq_len), BF16),
            jax.ShapeDtypeStruct((batch, seq_len, LANES), BF16),
            jax.ShapeDtypeStruct((t, LANES), F32),
            jax.ShapeDtypeStruct((batch, MLSTM_GATES, seq_len), F32),
        ],
        compiler_params=_cparams(2),
        name="ab_proj_attn",
    )(x, gain, w, cos, sin, qg, kg)


def _attn_kernel(q_ref, kt_ref, v_ref, o_ref, *, tq, tk, nk):
    rows = ATTN_HEADS * tq
    q = q_ref[0].reshape(rows, LANES)
    m = jnp.full((rows, 1), -jnp.inf, F32)
    l = jnp.zeros((rows, 1), F32)
    acc = jnp.zeros((rows, LANES), F32)
    for j in range(nk):
        s = _dot(q, kt_ref[0, :, j * tk:(j + 1) * tk])
        m_new = jnp.maximum(m, jnp.max(s, axis=-1, keepdims=True))
        alpha = jnp.exp2(m - m_new)
        p = jnp.exp2(s - m_new)
        l = alpha * l + jnp.sum(p, axis=-1, keepdims=True)
        acc = alpha * acc + _dot(p.astype(BF16), v_ref[0, j * tk:(j + 1) * tk, :])
        m = m_new
    o = acc / l
    lane = lax.broadcasted_iota(jnp.int32, (tq, LANES), 1)
    low = lane < ATTN_HEAD_DIM
    for p2 in range(ATTN_HEADS // 2):
        a = o[(2 * p2) * tq:(2 * p2 + 1) * tq]
        b = o[(2 * p2 + 1) * tq:(2 * p2 + 2) * tq]
        if (2 * p2) // (ATTN_HEADS // ATTN_KV_HEADS) == 0:
            pair = jnp.where(low, a, pltpu.roll(b, ATTN_HEAD_DIM, 1))
        else:
            pair = jnp.where(low, pltpu.roll(a, ATTN_HEAD_DIM, 1), b)
        o_ref[0, :, p2 * LANES:(p2 + 1) * LANES] = pair.astype(BF16)


def _attention(q, kt, v, batch, seq_len):
    tq = CHUNK
    tk = min(ATTN_K_TILE, seq_len)
    nk = seq_len // tk
    return pl.pallas_call(
        functools.partial(_attn_kernel, tq=tq, tk=tk, nk=nk),
        grid=(batch, seq_len // tq),
        in_specs=[
            pl.BlockSpec((1, ATTN_HEADS, tq, LANES), lambda b, i: (b, 0, i, 0)),
            pl.BlockSpec((1, LANES, seq_len), lambda b, i: (b, 0, 0)),
            pl.BlockSpec((1, seq_len, LANES), lambda b, i: (b, 0, 0)),
        ],
        out_specs=pl.BlockSpec((1, tq, ATTN_Q_WIDTH), lambda b, i: (b, i, 0)),
        out_shape=jax.ShapeDtypeStruct((batch, seq_len, ATTN_Q_WIDTH), BF16),
        compiler_params=_cparams(2),
        name="attention",
    )(q, kt, v)


def _dot3(x, m01, x_is_lhs=True):
    parts = _split3(x)
    if x_is_lhs:
        return _dot(parts[0], m01) + _dot(parts[1], m01) + _dot(parts[2], m01)
    return _dot(m01, parts[0]) + _dot(m01, parts[1]) + _dot(m01, parts[2])


def _mlstm_kernel(qvf_ref, ktf_ref, gf_ref, gtf_ref, qvb_ref, ktb_ref, gb_ref, gtb_ref, bias_ref, bias_t_ref,
                  hf_ref, hb_ref, cn_ref, m_ref, *, bb):
    c = pl.program_id(1)

    @pl.when(c == 0)
    def _():
        cn_ref[...] = jnp.zeros_like(cn_ref)
        m_ref[...] = jnp.zeros_like(m_ref)

    L = CHUNK
    dh = MLSTM_HEAD_DIM
    row = lax.broadcasted_iota(jnp.int32, (L, L), 0)
    col = lax.broadcasted_iota(jnp.int32, (L, L), 1)
    ones = jnp.ones((L, LANES), BF16)

    def chain(bi, d, qv_ref, kt_ref, g_ref, gt_ref, h_ref):
        allowed = (col <= row) if d == 0 else (col >= row)
        gates = g_ref[bi] + bias_ref[...]
        gates_t = gt_ref[bi] + bias_t_ref[...]
        logf = _log_sigmoid(gates)
        logf_t = _log_sigmoid(gates_t)
        sel_col = allowed.astype(BF16)
        sel_row = ((row <= col) if d == 0 else (row >= col)).astype(BF16)
        bcum = _dot3(logf, sel_col, x_is_lhs=False)
        bcum_t = _dot3(logf_t, sel_row)
        btot_t = _dot3(logf_t, ones)
        for hd in range(MLSTM_HEADS):
            idx = (bi * 2 + d) * MLSTM_HEADS + hd
            ig = 2 * MLSTM_HEADS * d + hd
            fg = ig + MLSTM_HEADS
            q = qv_ref[bi, :, hd * dh:(hd + 1) * dh]
            v = qv_ref[bi, :, MLSTM_WIDTH + hd * dh:MLSTM_WIDTH + (hd + 1) * dh]
            kt = kt_ref[bi, hd * dh:(hd + 1) * dh, :]
            v_ones = jnp.concatenate([v, ones], axis=1)
            a_row = gates_t[ig:ig + 1, :] - bcum_t[fg:fg + 1, :]
            b_last = btot_t[fg:fg + 1, :]
            b_col = _dot3(jnp.where(col == fg, bcum, 0.0), ones)
            m_old = m_ref[idx]
            cn_old = cn_ref[idx]
            dlog = jnp.where(allowed, b_col + a_row, -jnp.inf)
            inter = b_col + m_old
            m_row = jnp.maximum(inter, jnp.max(dlog, axis=-1, keepdims=True))
            w = jnp.exp(dlog - m_row) * _dot(q, kt)
            inter_w = jnp.exp(inter - m_row)
            intra = _dot(w.astype(BF16), v_ones)
            carry = _dot(q, cn_old.astype(BF16))
            num = intra[:, :dh] + inter_w * carry[:, :dh]
            den = intra[:, dh:] + inter_w * carry[:, dh:]
            h_ref[bi, :, hd * dh:(hd + 1) * dh] = num / jnp.maximum(jnp.abs(den), jnp.exp(-m_row))
            m_new = jnp.maximum(b_last + m_old, jnp.max(b_last[:, 0:1] + a_row, axis=-1, keepdims=True))
            keep = jnp.exp(b_last + m_old - m_new)
            wk_row = jnp.exp(b_last[:, 0:1] + a_row - m_new[:, 0:1])
            kw_t = (kt.astype(F32) * wk_row).astype(BF16)
            cn_ref[idx] = jnp.concatenate([keep, keep], axis=1) * cn_old + _dot(kw_t, v_ones)
            m_ref[idx] = m_new

    for bi in range(bb):
        chain(bi, 0, qvf_ref, ktf_ref, gf_ref, gtf_ref, hf_ref)
        chain(bi, 1, qvb_ref, ktb_ref, gb_ref, gtb_ref, hb_ref)


def _mlstm(qv, kt, gates, gt, bias, bias_t, batch, seq_len):
    nc = seq_len // CHUNK
    bb = min(MLSTM_BATCH_BLOCK, batch)
    qv = qv.reshape(batch, seq_len, qv.shape[-1])
    gates = gates.reshape(batch, seq_len, LANES)
    fwd = lambda b, c: (b, c, 0)
    bwd = lambda b, c: (b, nc - 1 - c, 0)
    fwd_t = lambda b, c: (b, 0, c)
    bwd_t = lambda b, c: (b, 0, nc - 1 - c)
    nstate = bb * 2 * MLSTM_HEADS
    hf, hb = pl.pallas_call(
        functools.partial(_mlstm_kernel, bb=bb),
        grid=(batch // bb, nc),
        in_specs=[
            pl.BlockSpec((bb, CHUNK, 2 * MLSTM_WIDTH), fwd),
            pl.BlockSpec((bb, MLSTM_WIDTH, CHUNK), fwd_t),
            pl.BlockSpec((bb, CHUNK, LANES), fwd),
            pl.BlockSpec((bb, MLSTM_GATES, CHUNK), fwd_t),
            pl.BlockSpec((bb, CHUNK, 2 * MLSTM_WIDTH), bwd),
            pl.BlockSpec((bb, MLSTM_WIDTH, CHUNK), bwd_t),
            pl.BlockSpec((bb, CHUNK, LANES), bwd),
            pl.BlockSpec((bb, MLSTM_GATES, CHUNK), bwd_t),
            pl.BlockSpec((1, LANES), lambda b, c: (0, 0)),
            pl.BlockSpec((MLSTM_GATES, LANES), lambda b, c: (0, 0)),
        ],
        out_specs=[
            pl.BlockSpec((bb, CHUNK, MLSTM_WIDTH), fwd),
            pl.BlockSpec((bb, CHUNK, MLSTM_WIDTH), bwd),
        ],
        out_shape=[jax.ShapeDtypeStruct((batch, seq_len, MLSTM_WIDTH), F32)] * 2,
        scratch_shapes=[
            pltpu.VMEM((nstate, MLSTM_HEAD_DIM, 2 * MLSTM_HEAD_DIM), F32),
            pltpu.VMEM((nstate, 1, LANES), F32),
        ],
        compiler_params=_cparams(2),
        name="mlstm",
    )(qv, kt, gates, gt, qv, kt, gates, gt, bias, bias_t)
    return hf.reshape(batch * seq_len, MLSTM_WIDTH), hb.reshape(batch * seq_len, MLSTM_WIDTH)


def _ab_out_kernel(a_ref, hf_ref, hb_ref, mo_ref, gain_ref, w_ref, x_ref, o_ref):
    acc = x_ref[...] + _dot(a_ref[...], w_ref[0:ATTN_Q_WIDTH, :])
    for hd in range(MLSTM_HEADS):
        sl = slice(hd * MLSTM_HEAD_DIM, (hd + 1) * MLSTM_HEAD_DIM)
        y = _rms(hf_ref[:, sl] + hb_ref[:, sl], gain_ref[:, sl])
        z = (y * jax.nn.sigmoid(mo_ref[:, sl])).astype(BF16)
        acc = acc + _dot(z, w_ref[ATTN_Q_WIDTH + hd * MLSTM_HEAD_DIM:ATTN_Q_WIDTH + (hd + 1) * MLSTM_HEAD_DIM, :])
    o_ref[...] = acc


def _ab_out(attn, hf, hb, mo, gain, w, x, seq_len):
    t, d = x.shape
    tm = min(ROW_TILE, seq_len)
    row = lambda i: (i, 0)
    fixed = lambda i: (0, 0)
    return pl.pallas_call(
        _ab_out_kernel,
        grid=(t // tm,),
        in_specs=[
            pl.BlockSpec((tm, ATTN_Q_WIDTH), row),
            pl.BlockSpec((tm, MLSTM_WIDTH), row),
            pl.BlockSpec((tm, MLSTM_WIDTH), row),
            pl.BlockSpec((tm, MLSTM_WIDTH), row),
            pl.BlockSpec((1, MLSTM_WIDTH), fixed),
            pl.BlockSpec(w.shape, fixed),
            pl.BlockSpec((tm, d), row),
        ],
        out_specs=pl.BlockSpec((tm, d), row),
        out_shape=jax.ShapeDtypeStruct((t, d), F32),
        compiler_params=_cparams(1),
        name="ab_out",
    )(attn, hf, hb, mo, gain, w, x)


def _rope_angles(seq_len, head_dim):
    rows = seq_len // GRID_W
    row_idx = jnp.repeat(jnp.arange(rows, dtype=F32), GRID_W)
    col_idx = jnp.tile(jnp.arange(GRID_W, dtype=F32), rows)
    axis_dim = head_dim // 2
    inv_freq = ROPE_THETA ** (-jnp.arange(0, axis_dim, 2, dtype=F32) / axis_dim)
    ang = jnp.concatenate([row_idx[:, None] * inv_freq, col_idx[:, None] * inv_freq], axis=-1)
    return jnp.cos(ang), jnp.sin(ang)


def _prepare(p):
    d = p["ab_w_in"].shape[1]
    n_ab = p["ab_w_in"].shape[0]
    a_cols = ATTN_Q_WIDTH + 2 * ATTN_KV_WIDTH
    m_lo = a_cols
    m_hi = a_cols + 4 * MLSTM_WIDTH
    w_in = p["ab_w_in"]
    pad = jnp.zeros((n_ab, d, LANES - MLSTM_GATES), w_in.dtype)
    out = dict(
        ab_w_attn=jnp.concatenate([w_in[:, :, :a_cols], w_in[:, :, m_hi:], pad], axis=-1).astype(BF16),
        ab_w_mlstm=w_in[:, :, m_lo:m_hi].astype(BF16),
        ab_bias=jnp.pad(p["ab_gate_bias"].astype(F32), ((0, 0), (0, LANES - MLSTM_GATES)))[:, None, :],
        ab_bias_t=jnp.broadcast_to(p["ab_gate_bias"].astype(F32)[:, :, None], (n_ab, MLSTM_GATES, LANES)),
        ab_w_out=p["ab_w_out"].astype(BF16),
        attn_qg=jnp.tile(p["attn_q_norm"].astype(F32), (1, LANES // ATTN_HEAD_DIM))[:, None, :],
        attn_kg=jnp.tile(p["attn_k_norm"].astype(F32), (1, LANES // ATTN_HEAD_DIM))[:, None, :],
        mlstm_gain=p["mlstm_out_norm"].astype(F32)[:, None, :],
        ret_w_in=p["ret_w_in"].astype(BF16),
        ret_w_out=p["ret_w_out"].astype(BF16),
        ret_gain=p["ret_out_norm"].astype(F32)[:, None, :],
        ret_decay_logit=p["ret_decay_logit"],
        norm_mix=p["norm_mix"].astype(F32)[:, None, :],
        norm_ffn=p["norm_ffn"].astype(F32)[:, None, :],
        norm_final=p["norm_final"].astype(F32)[None, :],
    )
    depth, _, ff2 = p["ffn_w_up"].shape
    dff = ff2 // 2
    nf = dff // FFN_COL_TILE
    up = p["ffn_w_up"].astype(BF16)
    tiles = lambda w: w.reshape(depth, d, nf, FFN_COL_TILE).transpose(0, 2, 1, 3)
    out["ffn_wu"] = tiles(up[:, :, :dff])
    out["ffn_wg"] = tiles(up[:, :, dff:])
    out["ffn_wd"] = p["ffn_w_down"].astype(BF16)
    cw = p["ffn_conv_w"].astype(F32).reshape(depth, 3, nf, FFN_COL_TILE).transpose(0, 2, 1, 3)
    out["ffn_cw"] = jnp.pad(cw, ((0, 0), (0, 0), (0, 5), (0, 0)))
    out["ffn_cb"] = p["ffn_conv_b"].astype(F32).reshape(depth, nf, 1, FFN_COL_TILE)
    return out


def _trunk(x3, w):
    batch, seq_len, d = x3.shape
    x = x3.reshape(batch * seq_len, d)
    cos_a, sin_a = _rope_angles(seq_len, ATTN_HEAD_DIM)
    reps = LANES // (ATTN_HEAD_DIM // 2)
    cos_a = jnp.tile(cos_a, (1, reps))
    sin_a = jnp.tile(jnp.concatenate([-sin_a, sin_a], axis=-1), (1, reps // 2))
    cos_r, sin_r = _rope_angles(seq_len, RET_QK_DIM)
    depth = w["norm_mix"].shape[0]
    for layer in range(depth):
        j = layer // 2
        if layer % 2 == 0:
            q, kt, v, gates, gt = _attn_proj(x, batch, seq_len, w["norm_mix"][layer], w["ab_w_attn"][j],
                                             cos_a, sin_a, w["attn_qg"][j], w["attn_kg"][j])
            qv, mkt, mo = _mlstm_proj(x, batch, seq_len, w["norm_mix"][layer], w["ab_w_mlstm"][j])
            attn = _attention(q, kt, v, batch, seq_len).reshape(batch * seq_len, ATTN_Q_WIDTH)
            hf, hb = _mlstm(qv, mkt, gates, gt, w["ab_bias"][j], w["ab_bias_t"][j], batch, seq_len)
            x = _ab_out(attn, hf, hb, mo, w["mlstm_gain"][j], w["ab_w_out"][j], x, seq_len)
        else:
            qk, v, g = _ret_proj(x, seq_len, w["norm_mix"][layer], w["ret_w_in"][j], cos_r, sin_r)
            tables = _ret_tables(w["ret_decay_logit"][j], min(RET_CHUNK, seq_len))
            yf, yb = _ret_core(qk, v, batch, seq_len, *tables)
            x = _ret_out(yf, yb, g, w["ret_gain"][j], w["ret_w_out"][j], x, seq_len)
        x = _ffn(x, seq_len, w["norm_ffn"][layer], w["ffn_wu"][layer], w["ffn_wg"][layer], w["ffn_cw"][layer],
                 w["ffn_cb"][layer], w["ffn_wd"][layer], w["norm_final"], final=(layer == depth - 1))
    return x.reshape(batch, seq_len, d)


def kernel(x_prompt, x_sample, norm_mix, norm_ffn, norm_final, ab_w_in, ab_gate_bias, attn_q_norm, attn_k_norm,
           mlstm_out_norm, ab_w_out, ret_w_in, ret_decay_logit, ret_out_norm, ret_w_out, ffn_w_up, ffn_conv_w,
           ffn_conv_b, ffn_w_down):
    w = _prepare(dict(
        norm_mix=norm_mix, norm_ffn=norm_ffn, norm_final=norm_final, ab_w_in=ab_w_in, ab_gate_bias=ab_gate_bias,
        attn_q_norm=attn_q_norm, attn_k_norm=attn_k_norm, mlstm_out_norm=mlstm_out_norm, ab_w_out=ab_w_out,
        ret_w_in=ret_w_in, ret_decay_logit=ret_decay_logit, ret_out_norm=ret_out_norm, ret_w_out=ret_w_out,
        ffn_w_up=ffn_w_up, ffn_conv_w=ffn_conv_w, ffn_conv_b=ffn_conv_b, ffn_w_down=ffn_w_down))
    return (_trunk(x_prompt, w), _trunk(x_sample, w))
```

```python
import functools

import numpy as np
import jax
import jax.numpy as jnp
from jax import lax
from jax.experimental import pallas as pl
from jax.experimental.pallas import tpu as pltpu

F32 = jnp.float32
BF16 = jnp.bfloat16

GRID_W = 64
ROPE_THETA = 10000.0
NORM_EPS = 1e-6
ATTN_HEADS = 8
ATTN_KV_HEADS = 2
ATTN_HEAD_DIM = 64
ATTN_Q_WIDTH = ATTN_HEADS * ATTN_HEAD_DIM
ATTN_KV_WIDTH = ATTN_KV_HEADS * ATTN_HEAD_DIM
MLSTM_HEADS = 4
MLSTM_HEAD_DIM = 128
MLSTM_WIDTH = MLSTM_HEADS * MLSTM_HEAD_DIM
MLSTM_GATES = 4 * MLSTM_HEADS
RET_HEADS = 4
RET_QK_DIM = 256
RET_V_DIM = 512
RET_QK_WIDTH = RET_HEADS * RET_QK_DIM
RET_V_WIDTH = RET_HEADS * RET_V_DIM
CHUNK = 128
RET_CHUNK = 256

LANES = 128
BF16_SUBLANES = 16
VMEM_LIMIT_BYTES = 56 * 1024 * 1024

ROW_TILE = 512
FFN_ROW_TILE = 1024
FFN_COL_TILE = 256
FFN_G_SLABS = 3
ATTN_K_TILE = 1024
MLSTM_BATCH_BLOCK = 4


def _cparams(n_axes):
    return pltpu.CompilerParams(
        dimension_semantics=("arbitrary",) * n_axes, vmem_limit_bytes=VMEM_LIMIT_BYTES)


def _rms(x, gain):
    ms = jnp.mean(x * x, axis=-1, keepdims=True)
    return x * lax.rsqrt(ms + NORM_EPS) * gain


def _dot(a, b):
    return jnp.dot(a, b, preferred_element_type=F32)


def _dot_nt(a, b):
    return lax.dot_general(a, b, (((1,), (1,)), ((), ())), preferred_element_type=F32)


def _dot_tn(a, b):
    return lax.dot_general(a, b, (((0,), (0,)), ((), ())), preferred_element_type=F32)


def _split3(x):
    x1 = x.astype(BF16)
    r1 = x - x1.astype(F32)
    x2 = r1.astype(BF16)
    x3 = (r1 - x2.astype(F32)).astype(BF16)
    return x1, x2, x3


def _log_sigmoid(x):
    return jnp.minimum(x, 0.0) - jnp.log1p(jnp.exp(-jnp.abs(x)))


def _ffn_kernel(xp_ref, x_ref, xn_ref, gain_ref, wu_ref, wg_ref, cw_ref, cb_ref, wd_ref, fin_ref,
                o_ref, h_ref, g_ref, a_ref, *, tm, nf, tf, blocks_per_seq, final):
    i = pl.program_id(0)
    halo = BF16_SUBLANES
    pos = i % blocks_per_seq
    gain = gain_ref[...]
    hp = jnp.where(pos == 0, 0.0, _rms(xp_ref[...], gain))
    hn = jnp.where(pos == blocks_per_seq - 1, 0.0, _rms(xn_ref[...], gain))
    x = x_ref[...]
    h_ref[0:halo, :] = hp.astype(BF16)
    h_ref[halo:halo + tm, :] = _rms(x, gain).astype(BF16)
    h_ref[halo + tm:, :] = hn.astype(BF16)
    sqrt_half = np.float32(np.sqrt(0.5))
    for j in range(nf):
        s = j % FFN_G_SLABS
        g_ref[s] = _dot(h_ref[...], wg_ref[j])
        cw = cw_ref[j]
        gc = (g_ref[s, halo - 1:halo - 1 + tm, :] * cw[0:1, :] + g_ref[s, halo:halo + tm, :] * cw[1:2, :]
              + g_ref[s, halo + 1:halo + 1 + tm, :] * cw[2:3, :] + cb_ref[j])
        u = _dot(h_ref[halo:halo + tm, :], wu_ref[j])
        act = 0.5 * gc * (1.0 + lax.erf(gc * sqrt_half))
        a_ref[:, j * tf:(j + 1) * tf] = (act * u).astype(BF16)
    y = x + _dot(a_ref[...], wd_ref[...])
    o_ref[...] = _rms(y, fin_ref[...]) if final else y


def _ffn(x, seq_len, gain, wu, wg, cw, cb, wd, fin_gain, final):
    t, d = x.shape
    tm = min(FFN_ROW_TILE, seq_len)
    nf, _, tf = wu.shape
    halo = BF16_SUBLANES
    nblk = t // tm
    hb = tm // halo
    const3 = lambda i: (0, 0, 0)
    kern = functools.partial(_ffn_kernel, tm=tm, nf=nf, tf=tf, blocks_per_seq=seq_len // tm, final=final)
    return pl.pallas_call(
        kern,
        grid=(nblk,),
        in_specs=[
            pl.BlockSpec((halo, d), lambda i: (jnp.maximum(i * hb - 1, 0), 0)),
            pl.BlockSpec((tm, d), lambda i: (i, 0)),
            pl.BlockSpec((halo, d), lambda i: (jnp.minimum((i + 1) * hb, nblk * hb - 1), 0)),
            pl.BlockSpec((1, d), lambda i: (0, 0)),
            pl.BlockSpec(wu.shape, const3, pipeline_mode=pl.Buffered(1)),
            pl.BlockSpec(wg.shape, const3, pipeline_mode=pl.Buffered(1)),
            pl.BlockSpec(cw.shape, const3),
            pl.BlockSpec(cb.shape, const3),
            pl.BlockSpec(wd.shape, lambda i: (0, 0), pipeline_mode=pl.Buffered(1)),
            pl.BlockSpec((1, d), lambda i: (0, 0)),
        ],
        out_specs=pl.BlockSpec((tm, d), lambda i: (i, 0)),
        out_shape=jax.ShapeDtypeStruct((t, d), F32),
        scratch_shapes=[
            pltpu.VMEM((tm + 2 * halo, d), BF16),
            pltpu.VMEM((FFN_G_SLABS, tm + 2 * halo, tf), F32),
            pltpu.VMEM((tm, nf * tf), BF16),
        ],
        compiler_params=_cparams(1),
        name="conv_ffn",
    )(x, x, x, gain, wu, wg, cw, cb, wd, fin_gain)


def _ret_proj_kernel(x_ref, gain_ref, w_ref, cos_ref, sin_ref, qk_ref, v_ref, g_ref):
    h = _rms(x_ref[...], gain_ref[...]).astype(BF16)
    c = cos_ref[...]
    s = sin_ref[...]
    half = RET_QK_DIM // 2
    for which, scale in ((0, 1.0), (1, RET_QK_DIM ** -0.5)):
        for hd in range(RET_HEADS):
            lo = which * RET_QK_WIDTH + hd * RET_QK_DIM
            acc = _dot(h, w_ref[:, lo:lo + RET_QK_DIM])
            x1 = acc[:, :half]
            x2 = acc[:, half:]
            r1 = x1 * c - x2 * s
            r2 = x2 * c + x1 * s
            if which == 1:
                r1 = r1 * np.float32(scale)
                r2 = r2 * np.float32(scale)
            qk_ref[:, lo:lo + half] = r1.astype(BF16)
            qk_ref[:, lo + half:lo + 2 * half] = r2.astype(BF16)
    v0 = 2 * RET_QK_WIDTH
    g0 = v0 + RET_V_WIDTH
    for hd in range(RET_HEADS):
        sl = slice(hd * RET_V_DIM, (hd + 1) * RET_V_DIM)
        v_ref[:, sl] = _dot(h, w_ref[:, v0 + hd * RET_V_DIM:v0 + (hd + 1) * RET_V_DIM]).astype(BF16)
        g_ref[:, sl] = _dot(h, w_ref[:, g0 + hd * RET_V_DIM:g0 + (hd + 1) * RET_V_DIM]).astype(BF16)


def _ret_proj(x, seq_len, gain, w, cos, sin):
    t, d = x.shape
    tm = min(ROW_TILE, seq_len)
    sblk = seq_len // tm
    row = lambda i: (i, 0)
    return pl.pallas_call(
        _ret_proj_kernel,
        grid=(t // tm,),
        in_specs=[
            pl.BlockSpec((tm, d), row),
            pl.BlockSpec((1, d), lambda i: (0, 0)),
            pl.BlockSpec(w.shape, lambda i: (0, 0), pipeline_mode=pl.Buffered(1)),
            pl.BlockSpec((tm, LANES), lambda i: (i % sblk, 0)),
            pl.BlockSpec((tm, LANES), lambda i: (i % sblk, 0)),
        ],
        out_specs=[
            pl.BlockSpec((tm, 2 * RET_QK_WIDTH), row),
            pl.BlockSpec((tm, RET_V_WIDTH), row),
            pl.BlockSpec((tm, RET_V_WIDTH), row),
        ],
        out_shape=[
            jax.ShapeDtypeStruct((t, 2 * RET_QK_WIDTH), BF16),
            jax.ShapeDtypeStruct((t, RET_V_WIDTH), BF16),
            jax.ShapeDtypeStruct((t, RET_V_WIDTH), BF16),
        ],
        compiler_params=_cparams(1),
        name="ret_proj",
    )(x, gain, w, cos, sin)


def _ret_kernel(qkf_ref, vf_ref, qkb_ref, vb_ref, intra_ref, qd_ref, kd_ref, cd_ref,
                yf_ref, yb_ref, s_ref):
    c = pl.program_id(1)

    @pl.when(c == 0)
    def _():
        s_ref[...] = jnp.zeros_like(s_ref)

    for d, (qk_ref, v_ref, y_ref) in enumerate(((qkf_ref, vf_ref, yf_ref), (qkb_ref, vb_ref, yb_ref))):
        for hd in range(RET_HEADS):
            idx = d * RET_HEADS + hd
            q = qk_ref[:, hd * RET_QK_DIM:(hd + 1) * RET_QK_DIM]
            k = qk_ref[:, RET_QK_WIDTH + hd * RET_QK_DIM:RET_QK_WIDTH + (hd + 1) * RET_QK_DIM]
            v = v_ref[:, hd * RET_V_DIM:(hd + 1) * RET_V_DIM]
            state = s_ref[idx]
            scores = _dot_nt(q, k) * intra_ref[d, hd]
            inter = _dot(q, state.astype(BF16)) * qd_ref[d, hd][:, 0:1]
            y_ref[:, hd * RET_V_DIM:(hd + 1) * RET_V_DIM] = (_dot(scores.astype(BF16), v) + inter).astype(BF16)
            kdec = (k.astype(F32) * kd_ref[d, hd][:, 0:1]).astype(BF16)
            s_ref[idx] = cd_ref[d, hd][0:1, 0:1] * state + _dot_tn(kdec, v)


def _ret_core(qk, v, batch, seq_len, intra, qd, kd, cd):
    t = qk.shape[0]
    chunk = intra.shape[-1]
    nc = seq_len // chunk
    fwd = lambda b, c: (b * nc + c, 0)
    bwd = lambda b, c: (b * nc + nc - 1 - c, 0)
    const4 = lambda b, c: (0, 0, 0, 0)
    return pl.pallas_call(
        _ret_kernel,
        grid=(batch, nc),
        in_specs=[
            pl.BlockSpec((chunk, 2 * RET_QK_WIDTH), fwd),
            pl.BlockSpec((chunk, RET_V_WIDTH), fwd),
            pl.BlockSpec((chunk, 2 * RET_QK_WIDTH), bwd),
            pl.BlockSpec((chunk, RET_V_WIDTH), bwd),
            pl.BlockSpec(intra.shape, const4),
            pl.BlockSpec(qd.shape, const4),
            pl.BlockSpec(kd.shape, const4),
            pl.BlockSpec(cd.shape, const4),
        ],
        out_specs=[
            pl.BlockSpec((chunk, RET_V_WIDTH), fwd),
            pl.BlockSpec((chunk, RET_V_WIDTH), bwd),
        ],
        out_shape=[jax.ShapeDtypeStruct((t, RET_V_WIDTH), BF16)] * 2,
        scratch_shapes=[pltpu.VMEM((2 * RET_HEADS, RET_QK_DIM, RET_V_DIM), F32)],
        compiler_params=_cparams(2),
        name="retention",
    )(qk, v, qk, v, intra, qd, kd, cd)


def _ret_out_kernel(yf_ref, yb_ref, g_ref, gain_ref, w_ref, x_ref, o_ref):
    acc = x_ref[...]
    for hd in range(RET_HEADS):
        sl = slice(hd * RET_V_DIM, (hd + 1) * RET_V_DIM)
        y = _rms(yf_ref[:, sl].astype(F32) + yb_ref[:, sl].astype(F32), gain_ref[:, sl])
        g = g_ref[:, sl].astype(F32)
        z = (y * (g * jax.nn.sigmoid(g))).astype(BF16)
        acc = acc + _dot(z, w_ref[sl, :])
    o_ref[...] = acc


def _ret_out(yf, yb, g, gain, w, x, seq_len):
    t, d = x.shape
    tm = min(ROW_TILE, seq_len)
    row = lambda i: (i, 0)
    fixed = lambda i: (0, 0)
    return pl.pallas_call(
        _ret_out_kernel,
        grid=(t // tm,),
        in_specs=[
            pl.BlockSpec((tm, RET_V_WIDTH), row),
            pl.BlockSpec((tm, RET_V_WIDTH), row),
            pl.BlockSpec((tm, RET_V_WIDTH), row),
            pl.BlockSpec((1, RET_V_WIDTH), fixed),
            pl.BlockSpec((RET_V_WIDTH, d), fixed),
            pl.BlockSpec((tm, d), row),
        ],
        out_specs=pl.BlockSpec((tm, d), row),
        out_shape=jax.ShapeDtypeStruct((t, d), F32),
        compiler_params=_cparams(1),
        name="ret_out",
    )(yf, yb, g, gain, w, x)


def _ret_tables(decay_logit, chunk):
    log_gamma = jax.nn.log_sigmoid(decay_logit.astype(F32))
    pos = jnp.arange(chunk, dtype=F32)
    diff = pos[:, None] - pos[None, :]
    lg = log_gamma[:, :, None, None]
    tri = jnp.exp(jnp.maximum(diff, 0.0)[None, None] * lg)
    intra_f = jnp.where((diff >= 0)[None], tri[0], 0.0)
    intra_b = jnp.where((diff > 0)[None], tri[1], 0.0)
    intra_b = intra_b[:, ::-1, ::-1]
    qdec = jnp.exp((pos + 1.0)[None, None, :] * log_gamma[:, :, None])
    kdec = jnp.exp((chunk - 1.0 - pos)[None, None, :] * log_gamma[:, :, None])
    qdec = jnp.stack([qdec[0], qdec[1, :, ::-1]])
    kdec = jnp.stack([kdec[0], kdec[1, :, ::-1]])
    cdec = jnp.exp(chunk * log_gamma)
    rep = lambda a: jnp.broadcast_to(a[..., None], a.shape + (LANES,))
    return (jnp.stack([intra_f, intra_b]), rep(qdec), rep(kdec),
            jnp.broadcast_to(cdec[:, :, None, None], cdec.shape + (8, LANES)))


def _ab_proj_kernel(x_ref, gain_ref, w_ref, wm_ref, cos_ref, sin_ref, qg_ref, kg_ref,
                    q_ref, kt_ref, v_ref, g_ref, gt_ref, mqv_ref, mkt_ref, mo_ref, *, tm):
    h = _rms(x_ref[...], gain_ref[...]).astype(BF16)
    lane = lax.broadcasted_iota(jnp.int32, (tm, LANES), 1)
    ri = lax.broadcasted_iota(jnp.int32, (LANES, LANES), 0)
    ci = lax.broadcasted_iota(jnp.int32, (LANES, LANES), 1)
    same_head = ((ri // ATTN_HEAD_DIM) == (ci // ATTN_HEAD_DIM)).astype(BF16)

    def head_sumsq(x):
        x2 = x * x
        hi = x2.astype(BF16)
        lo = (x2 - hi.astype(F32)).astype(BF16)
        return _dot(hi, same_head) + _dot(lo, same_head)

    pa = _dot(h, w_ref[...])
    nqk = (ATTN_Q_WIDTH + ATTN_KV_WIDTH) // LANES
    sumsq = [head_sumsq(pa[:, p * LANES:(p + 1) * LANES]) for p in range(nqk)]
    n = MLSTM_WIDTH
    mqv_ref[:, 0:n] = _dot(h, wm_ref[:, 0:n]).astype(BF16)
    mkt_ref[0] = (_dot(h, wm_ref[:, n:2 * n]) * np.float32(MLSTM_HEAD_DIM ** -0.5)).T.astype(BF16)
    mqv_ref[:, n:2 * n] = _dot(h, wm_ref[:, 2 * n:3 * n]).astype(BF16)
    mo_ref[...] = _dot(h, wm_ref[:, 3 * n:4 * n]).astype(BF16)
    gates = pa[:, ATTN_Q_WIDTH + 2 * LANES:]
    g_ref[...] = gates
    gt_ref[0] = gates.T[0:MLSTM_GATES, :]
    c = cos_ref[...]
    s = sin_ref[...]
    first_half = (lane % ATTN_HEAD_DIM) < (ATTN_HEAD_DIM // 2)
    low = lane < ATTN_HEAD_DIM

    def norm_rope(p, gain):
        x = pa[:, p * LANES:(p + 1) * LANES]
        xn = x * lax.rsqrt(sumsq[p] * np.float32(1.0 / ATTN_HEAD_DIM) + NORM_EPS) * gain
        swapped = jnp.where(first_half, pltpu.roll(xn, LANES - ATTN_HEAD_DIM // 2, 1),
                            pltpu.roll(xn, ATTN_HEAD_DIM // 2, 1))
        return xn * c + swapped * s

    for p in range(ATTN_HEADS // 2):
        y = norm_rope(p, qg_ref[...]) * np.float32(ATTN_HEAD_DIM ** -0.5 * np.log2(np.e))
        yr = pltpu.roll(y, ATTN_HEAD_DIM, 1)
        if (2 * p) // (ATTN_HEADS // ATTN_KV_HEADS) == 0:
            head_a, head_b = jnp.where(low, y, 0.0), jnp.where(low, yr, 0.0)
        else:
            head_a, head_b = jnp.where(low, 0.0, yr), jnp.where(low, 0.0, y)
        q_ref[0, 2 * p] = head_a.astype(BF16)
        q_ref[0, 2 * p + 1] = head_b.astype(BF16)
    k = norm_rope(nqk - 1, kg_ref[...])
    kt_ref[0] = k.T.astype(BF16)
    v_ref[0] = pa[:, ATTN_Q_WIDTH + LANES:ATTN_Q_WIDTH + 2 * LANES].astype(BF16)


def _ab_proj(x, batch, seq_len, gain, w, wm, cos, sin, qg, kg):
    t, d = x.shape
    tm = min(ROW_TILE, seq_len)
    sblk = seq_len // tm
    rows = lambda b, s: (b * sblk + s, 0)
    return pl.pallas_call(
        functools.partial(_ab_proj_kernel, tm=tm),
        grid=(batch, sblk),
        in_specs=[
            pl.BlockSpec((tm, d), rows),
            pl.BlockSpec((1, d), lambda b, s: (0, 0)),
            pl.BlockSpec(w.shape, lambda b, s: (0, 0), pipeline_mode=pl.Buffered(1)),
            pl.BlockSpec(wm.shape, lambda b, s: (0, 0), pipeline_mode=pl.Buffered(1)),
            pl.BlockSpec((tm, LANES), lambda b, s: (s, 0)),
            pl.BlockSpec((tm, LANES), lambda b, s: (s, 0)),
            pl.BlockSpec((1, LANES), lambda b, s: (0, 0)),
            pl.BlockSpec((1, LANES), lambda b, s: (0, 0)),
        ],
        out_specs=[
            pl.BlockSpec((1, ATTN_HEADS, tm, LANES), lambda b, s: (b, 0, s, 0)),
            pl.BlockSpec((1, LANES, tm), lambda b, s: (b, 0, s)),
            pl.BlockSpec((1, tm, LANES), lambda b, s: (b, s, 0)),
            pl.BlockSpec((tm, LANES), rows),
            pl.BlockSpec((1, MLSTM_GATES, tm), lambda b, s: (b, 0, s)),
            pl.BlockSpec((tm, 2 * MLSTM_WIDTH), rows),
            pl.BlockSpec((1, MLSTM_WIDTH, tm), lambda b, s: (b, 0, s)),
            pl.BlockSpec((tm, MLSTM_WIDTH), rows),
        ],
        out_shape=[
            jax.ShapeDtypeStruct((batch, ATTN_HEADS, seq_len, LANES), BF16),
            jax.ShapeDtypeStruct((batch, LANES, seq_len), BF16),
            jax.ShapeDtypeStruct((batch, seq_len, LANES), BF16),
            jax.ShapeDtypeStruct((t, LANES), F32),
            jax.ShapeDtypeStruct((batch, MLSTM_GATES, seq_len), F32),
            jax.ShapeDtypeStruct((t, 2 * MLSTM_WIDTH), BF16),
            jax.ShapeDtypeStruct((batch, MLSTM_WIDTH, seq_len), BF16),
            jax.ShapeDtypeStruct((t, MLSTM_WIDTH), BF16),
        ],
        compiler_params=_cparams(2),
        name="ab_proj",
    )(x, gain, w, wm, cos, sin, qg, kg)


def _attn_kernel(q_ref, kt_ref, v_ref, o_ref, *, tq, tk, nk):
    rows = ATTN_HEADS * tq
    q = q_ref[0].reshape(rows, LANES)
    m = jnp.full((rows, 1), -jnp.inf, F32)
    l = jnp.zeros((rows, 1), F32)
    acc = jnp.zeros((rows, LANES), F32)
    for j in range(nk):
        s = _dot(q, kt_ref[0, :, j * tk:(j + 1) * tk])
        m_new = jnp.maximum(m, jnp.max(s, axis=-1, keepdims=True))
        alpha = jnp.exp2(m - m_new)
        p = jnp.exp2(s - m_new)
        l = alpha * l + jnp.sum(p, axis=-1, keepdims=True)
        acc = alpha * acc + _dot(p.astype(BF16), v_ref[0, j * tk:(j + 1) * tk, :])
        m = m_new
    o = acc / l
    lane = lax.broadcasted_iota(jnp.int32, (tq, LANES), 1)
    low = lane < ATTN_HEAD_DIM
    for p2 in range(ATTN_HEADS // 2):
        a = o[(2 * p2) * tq:(2 * p2 + 1) * tq]
        b = o[(2 * p2 + 1) * tq:(2 * p2 + 2) * tq]
        if (2 * p2) // (ATTN_HEADS // ATTN_KV_HEADS) == 0:
            pair = jnp.where(low, a, pltpu.roll(b, ATTN_HEAD_DIM, 1))
        else:
            pair = jnp.where(low, pltpu.roll(a, ATTN_HEAD_DIM, 1), b)
        o_ref[0, :, p2 * LANES:(p2 + 1) * LANES] = pair.astype(BF16)


def _attention(q, kt, v, batch, seq_len):
    tq = CHUNK
    tk = min(ATTN_K_TILE, seq_len)
    nk = seq_len // tk
    return pl.pallas_call(
        functools.partial(_attn_kernel, tq=tq, tk=tk, nk=nk),
        grid=(batch, seq_len // tq),
        in_specs=[
            pl.BlockSpec((1, ATTN_HEADS, tq, LANES), lambda b, i: (b, 0, i, 0)),
            pl.BlockSpec((1, LANES, seq_len), lambda b, i: (b, 0, 0)),
            pl.BlockSpec((1, seq_len, LANES), lambda b, i: (b, 0, 0)),
        ],
        out_specs=pl.BlockSpec((1, tq, ATTN_Q_WIDTH), lambda b, i: (b, i, 0)),
        out_shape=jax.ShapeDtypeStruct((batch, seq_len, ATTN_Q_WIDTH), BF16),
        compiler_params=_cparams(2),
        name="attention",
    )(q, kt, v)


def _dot3(x, m01, x_is_lhs=True):
    parts = _split3(x)
    if x_is_lhs:
        return _dot(parts[0], m01) + _dot(parts[1], m01) + _dot(parts[2], m01)
    return _dot(m01, parts[0]) + _dot(m01, parts[1]) + _dot(m01, parts[2])


def _mlstm_kernel(qvf_ref, ktf_ref, gf_ref, gtf_ref, qvb_ref, ktb_ref, gb_ref, gtb_ref, bias_ref, bias_t_ref,
                  hf_ref, hb_ref, cn_ref, m_ref, *, bb):
    c = pl.program_id(1)

    @pl.when(c == 0)
    def _():
        cn_ref[...] = jnp.zeros_like(cn_ref)
        m_ref[...] = jnp.zeros_like(m_ref)

    L = CHUNK
    dh = MLSTM_HEAD_DIM
    row = lax.broadcasted_iota(jnp.int32, (L, L), 0)
    col = lax.broadcasted_iota(jnp.int32, (L, L), 1)
    ones = jnp.ones((L, LANES), BF16)

    def chain(bi, d, qv_ref, kt_ref, g_ref, gt_ref, h_ref):
        allowed = (col <= row) if d == 0 else (col >= row)
        gates = g_ref[bi] + bias_ref[...]
        gates_t = gt_ref[bi] + bias_t_ref[...]
        logf = _log_sigmoid(gates)
        logf_t = _log_sigmoid(gates_t)
        sel_col = allowed.astype(BF16)
        sel_row = ((row <= col) if d == 0 else (row >= col)).astype(BF16)
        bcum = _dot3(logf, sel_col, x_is_lhs=False)
        bcum_t = _dot3(logf_t, sel_row)
        last = L - 1 if d == 0 else 0
        for hd in range(MLSTM_HEADS):
            idx = (bi * 2 + d) * MLSTM_HEADS + hd
            ig = 2 * MLSTM_HEADS * d + hd
            fg = ig + MLSTM_HEADS
            q = qv_ref[bi, :, hd * dh:(hd + 1) * dh]
            v = qv_ref[bi, :, MLSTM_WIDTH + hd * dh:MLSTM_WIDTH + (hd + 1) * dh]
            kt = kt_ref[bi, hd * dh:(hd + 1) * dh, :]
            v_ones = jnp.concatenate([v, ones], axis=1)
            a_row = gates_t[ig:ig + 1, :] - bcum_t[fg:fg + 1, :]
            b_last = jnp.broadcast_to(bcum_t[fg:fg + 1, last:last + 1], (1, LANES))
            b_col = jnp.broadcast_to(bcum[:, fg:fg + 1], (L, LANES))
            m_old = m_ref[idx]
            cn_old = cn_ref[idx]
            dlog = jnp.where(allowed, b_col + a_row, -jnp.inf)
            inter = b_col + m_old
            m_row = jnp.maximum(inter, jnp.max(dlog, axis=-1, keepdims=True))
            w = jnp.exp(dlog - m_row) * _dot(q, kt)
            inter_w = jnp.exp(inter - m_row)
            intra = _dot(w.astype(BF16), v_ones)
            carry = _dot(q, cn_old.astype(BF16))
            num = intra[:, :dh] + inter_w * carry[:, :dh]
            den = intra[:, dh:] + inter_w * carry[:, dh:]
            hval = num / jnp.maximum(jnp.abs(den), jnp.exp(-m_row))
            h_ref[bi, :, hd * dh:(hd + 1) * dh] = hval.astype(BF16)
            m_new = jnp.maximum(b_last + m_old, jnp.max(b_last[:, 0:1] + a_row, axis=-1, keepdims=True))
            keep = jnp.exp(b_last + m_old - m_new)
            wk_row = jnp.exp(b_last[:, 0:1] + a_row - m_new[:, 0:1])
            kw_t = (kt.astype(F32) * wk_row).astype(BF16)
            cn_ref[idx] = jnp.concatenate([keep, keep], axis=1) * cn_old + _dot(kw_t, v_ones)
            m_ref[idx] = m_new

    for bi in range(bb):
        chain(bi, 0, qvf_ref, ktf_ref, gf_ref, gtf_ref, hf_ref)
        chain(bi, 1, qvb_ref, ktb_ref, gb_ref, gtb_ref, hb_ref)


def _mlstm(qv, kt, gates, gt, bias, bias_t, batch, seq_len):
    nc = seq_len // CHUNK
    bb = min(MLSTM_BATCH_BLOCK, batch)
    qv = qv.reshape(batch, seq_len, qv.shape[-1])
    gates = gates.reshape(batch, seq_len, LANES)
    fwd = lambda b, c: (b, c, 0)
    bwd = lambda b, c: (b, nc - 1 - c, 0)
    fwd_t = lambda b, c: (b, 0, c)
    bwd_t = lambda b, c: (b, 0, nc - 1 - c)
    nstate = bb * 2 * MLSTM_HEADS
    hf, hb = pl.pallas_call(
        functools.partial(_mlstm_kernel, bb=bb),
        grid=(batch // bb, nc),
        in_specs=[
            pl.BlockSpec((bb, CHUNK, 2 * MLSTM_WIDTH), fwd),
            pl.BlockSpec((bb, MLSTM_WIDTH, CHUNK), fwd_t),
            pl.BlockSpec((bb, CHUNK, LANES), fwd),
            pl.BlockSpec((bb, MLSTM_GATES, CHUNK), fwd_t),
            pl.BlockSpec((bb, CHUNK, 2 * MLSTM_WIDTH), bwd),
            pl.BlockSpec((bb, MLSTM_WIDTH, CHUNK), bwd_t),
            pl.BlockSpec((bb, CHUNK, LANES), bwd),
            pl.BlockSpec((bb, MLSTM_GATES, CHUNK), bwd_t),
            pl.BlockSpec((1, LANES), lambda b, c: (0, 0)),
            pl.BlockSpec((MLSTM_GATES, LANES), lambda b, c: (0, 0)),
        ],
        out_specs=[
            pl.BlockSpec((bb, CHUNK, MLSTM_WIDTH), fwd),
            pl.BlockSpec((bb, CHUNK, MLSTM_WIDTH), bwd),
        ],
        out_shape=[jax.ShapeDtypeStruct((batch, seq_len, MLSTM_WIDTH), BF16)] * 2,
        scratch_shapes=[
            pltpu.VMEM((nstate, MLSTM_HEAD_DIM, 2 * MLSTM_HEAD_DIM), F32),
            pltpu.VMEM((nstate, 1, LANES), F32),
        ],
        compiler_params=_cparams(2),
        name="mlstm",
    )(qv, kt, gates, gt, qv, kt, gates, gt, bias, bias_t)
    return hf.reshape(batch * seq_len, MLSTM_WIDTH), hb.reshape(batch * seq_len, MLSTM_WIDTH)


def _ab_out_kernel(a_ref, hf_ref, hb_ref, mo_ref, gain_ref, w_ref, x_ref, o_ref):
    acc = x_ref[...] + _dot(a_ref[...], w_ref[0:ATTN_Q_WIDTH, :])
    for hd in range(MLSTM_HEADS):
        sl = slice(hd * MLSTM_HEAD_DIM, (hd + 1) * MLSTM_HEAD_DIM)
        y = _rms(hf_ref[:, sl].astype(F32) + hb_ref[:, sl].astype(F32), gain_ref[:, sl])
        z = (y * jax.nn.sigmoid(mo_ref[:, sl].astype(F32))).astype(BF16)
        acc = acc + _dot(z, w_ref[ATTN_Q_WIDTH + hd * MLSTM_HEAD_DIM:ATTN_Q_WIDTH + (hd + 1) * MLSTM_HEAD_DIM, :])
    o_ref[...] = acc


def _ab_out(attn, hf, hb, mo, gain, w, x, seq_len):
    t, d = x.shape
    tm = min(ROW_TILE, seq_len)
    row = lambda i: (i, 0)
    fixed = lambda i: (0, 0)
    return pl.pallas_call(
        _ab_out_kernel,
        grid=(t // tm,),
        in_specs=[
            pl.BlockSpec((tm, ATTN_Q_WIDTH), row),
            pl.BlockSpec((tm, MLSTM_WIDTH), row),
            pl.BlockSpec((tm, MLSTM_WIDTH), row),
            pl.BlockSpec((tm, MLSTM_WIDTH), row),
            pl.BlockSpec((1, MLSTM_WIDTH), fixed),
            pl.BlockSpec(w.shape, fixed),
            pl.BlockSpec((tm, d), row),
        ],
        out_specs=pl.BlockSpec((tm, d), row),
        out_shape=jax.ShapeDtypeStruct((t, d), F32),
        compiler_params=_cparams(1),
        name="ab_out",
    )(attn, hf, hb, mo, gain, w, x)


def _rope_angles(seq_len, head_dim):
    rows = seq_len // GRID_W
    row_idx = jnp.repeat(jnp.arange(rows, dtype=F32), GRID_W)
    col_idx = jnp.tile(jnp.arange(GRID_W, dtype=F32), rows)
    axis_dim = head_dim // 2
    inv_freq = ROPE_THETA ** (-jnp.arange(0, axis_dim, 2, dtype=F32) / axis_dim)
    ang = jnp.concatenate([row_idx[:, None] * inv_freq, col_idx[:, None] * inv_freq], axis=-1)
    return jnp.cos(ang), jnp.sin(ang)


def _prepare(p):
    d = p["ab_w_in"].shape[1]
    n_ab = p["ab_w_in"].shape[0]
    a_cols = ATTN_Q_WIDTH + 2 * ATTN_KV_WIDTH
    m_lo = a_cols
    m_hi = a_cols + 4 * MLSTM_WIDTH
    w_in = p["ab_w_in"]
    pad = jnp.zeros((n_ab, d, LANES - MLSTM_GATES), w_in.dtype)
    out = dict(
        ab_w_attn=jnp.concatenate([w_in[:, :, :a_cols], w_in[:, :, m_hi:], pad], axis=-1).astype(BF16),
        ab_w_mlstm=w_in[:, :, m_lo:m_hi].astype(BF16),
        ab_bias=jnp.pad(p["ab_gate_bias"].astype(F32), ((0, 0), (0, LANES - MLSTM_GATES)))[:, None, :],
        ab_bias_t=jnp.broadcast_to(p["ab_gate_bias"].astype(F32)[:, :, None], (n_ab, MLSTM_GATES, LANES)),
        ab_w_out=p["ab_w_out"].astype(BF16),
        attn_qg=jnp.tile(p["attn_q_norm"].astype(F32), (1, LANES // ATTN_HEAD_DIM))[:, None, :],
        attn_kg=jnp.tile(p["attn_k_norm"].astype(F32), (1, LANES // ATTN_HEAD_DIM))[:, None, :],
        mlstm_gain=p["mlstm_out_norm"].astype(F32)[:, None, :],
        ret_w_in=p["ret_w_in"].astype(BF16),
        ret_w_out=p["ret_w_out"].astype(BF16),
        ret_gain=p["ret_out_norm"].astype(F32)[:, None, :],
        ret_decay_logit=p["ret_decay_logit"],
        norm_mix=p["norm_mix"].astype(F32)[:, None, :],
        norm_ffn=p["norm_ffn"].astype(F32)[:, None, :],
        norm_final=p["norm_final"].astype(F32)[None, :],
    )
    depth, _, ff2 = p["ffn_w_up"].shape
    dff = ff2 // 2
    nf = dff // FFN_COL_TILE
    up = p["ffn_w_up"].astype(BF16)
    tiles = lambda w: w.reshape(depth, d, nf, FFN_COL_TILE).transpose(0, 2, 1, 3)
    out["ffn_wu"] = tiles(up[:, :, :dff])
    out["ffn_wg"] = tiles(up[:, :, dff:])
    out["ffn_wd"] = p["ffn_w_down"].astype(BF16)
    cw = p["ffn_conv_w"].astype(F32).reshape(depth, 3, nf, FFN_COL_TILE).transpose(0, 2, 1, 3)
    out["ffn_cw"] = jnp.pad(cw, ((0, 0), (0, 0), (0, 5), (0, 0)))
    out["ffn_cb"] = p["ffn_conv_b"].astype(F32).reshape(depth, nf, 1, FFN_COL_TILE)
    return out


def _trunk(x3, w):
    batch, seq_len, d = x3.shape
    x = x3.reshape(batch * seq_len, d)
    cos_a, sin_a = _rope_angles(seq_len, ATTN_HEAD_DIM)
    reps = LANES // (ATTN_HEAD_DIM // 2)
    cos_a = jnp.tile(cos_a, (1, reps))
    sin_a = jnp.tile(jnp.concatenate([-sin_a, sin_a], axis=-1), (1, reps // 2))
    cos_r, sin_r = _rope_angles(seq_len, RET_QK_DIM)
    depth = w["norm_mix"].shape[0]
    for layer in range(depth):
        j = layer // 2
        if layer % 2 == 0:
            q, kt, v, gates, gt, qv, mkt, mo = _ab_proj(
                x, batch, seq_len, w["norm_mix"][layer], w["ab_w_attn"][j], w["ab_w_mlstm"][j],
                cos_a, sin_a, w["attn_qg"][j], w["attn_kg"][j])
            attn = _attention(q, kt, v, batch, seq_len).reshape(batch * seq_len, ATTN_Q_WIDTH)
            hf, hb = _mlstm(qv, mkt, gates, gt, w["ab_bias"][j], w["ab_bias_t"][j], batch, seq_len)
            x = _ab_out(attn, hf, hb, mo, w["mlstm_gain"][j], w["ab_w_out"][j], x, seq_len)
        else:
            qk, v, g = _ret_proj(x, seq_len, w["norm_mix"][layer], w["ret_w_in"][j], cos_r, sin_r)
            tables = _ret_tables(w["ret_decay_logit"][j], min(RET_CHUNK, seq_len))
            yf, yb = _ret_core(qk, v, batch, seq_len, *tables)
            x = _ret_out(yf, yb, g, w["ret_gain"][j], w["ret_w_out"][j], x, seq_len)
        x = _ffn(x, seq_len, w["norm_ffn"][layer], w["ffn_wu"][layer], w["ffn_wg"][layer], w["ffn_cw"][layer],
                 w["ffn_cb"][layer], w["ffn_wd"][layer], w["norm_final"], final=(layer == depth - 1))
    return x.reshape(batch, seq_len, d)


def kernel(x_prompt, x_sample, norm_mix, norm_ffn, norm_final, ab_w_in, ab_gate_bias, attn_q_norm, attn_k_norm,
           mlstm_out_norm, ab_w_out, ret_w_in, ret_decay_logit, ret_out_norm, ret_w_out, ffn_w_up, ffn_conv_w,
           ffn_conv_b, ffn_w_down):
    w = _prepare(dict(
        norm_mix=norm_mix, norm_ffn=norm_ffn, norm_final=norm_final, ab_w_in=ab_w_in, ab_gate_bias=ab_gate_bias,
        attn_q_norm=attn_q_norm, attn_k_norm=attn_k_norm, mlstm_out_norm=mlstm_out_norm, ab_w_out=ab_w_out,
        ret_w_in=ret_w_in, ret_decay_logit=ret_decay_logit, ret_out_norm=ret_out_norm, ret_w_out=ret_w_out,
        ffn_w_up=ffn_w_up, ffn_conv_w=ffn_conv_w, ffn_conv_b=ffn_conv_b, ffn_w_down=ffn_w_down))
    return (_trunk(x_prompt, w), _trunk(x_sample, w))
```

```python
import functools

import numpy as np
import jax
import jax.numpy as jnp
from jax import lax
from jax.experimental import pallas as pl
from jax.experimental.pallas import tpu as pltpu

F32 = jnp.float32
BF16 = jnp.bfloat16

GRID_W = 64
ROPE_THETA = 10000.0
NORM_EPS = 1e-6
ATTN_HEADS = 8
ATTN_KV_HEADS = 2
ATTN_HEAD_DIM = 64
ATTN_Q_WIDTH = ATTN_HEADS * ATTN_HEAD_DIM
ATTN_KV_WIDTH = ATTN_KV_HEADS * ATTN_HEAD_DIM
MLSTM_HEADS = 4
MLSTM_HEAD_DIM = 128
MLSTM_WIDTH = MLSTM_HEADS * MLSTM_HEAD_DIM
MLSTM_GATES = 4 * MLSTM_HEADS
RET_HEADS = 4
RET_QK_DIM = 256
RET_V_DIM = 512
RET_QK_WIDTH = RET_HEADS * RET_QK_DIM
RET_V_WIDTH = RET_HEADS * RET_V_DIM
CHUNK = 128
RET_CHUNK = 256

LANES = 128
BF16_SUBLANES = 16
VMEM_LIMIT_BYTES = 56 * 1024 * 1024

ROW_TILE = 512
FFN_ROW_TILE = 1024
FFN_COL_TILE = 256
FFN_G_SLABS = 3
ATTN_Q_TILE = 256
ATTN_K_TILE = 2048
ATTN_SCORE_BYTES = 32 * 1024 * 1024
MLSTM_BATCH_BLOCK = 4


def _cparams(n_axes):
    return pltpu.CompilerParams(
        dimension_semantics=("arbitrary",) * n_axes, vmem_limit_bytes=VMEM_LIMIT_BYTES)


def _rms(x, gain):
    ms = jnp.mean(x * x, axis=-1, keepdims=True)
    return x * lax.rsqrt(ms + NORM_EPS) * gain


def _dot(a, b):
    return jnp.dot(a, b, preferred_element_type=F32)


def _dot_nt(a, b):
    return lax.dot_general(a, b, (((1,), (1,)), ((), ())), preferred_element_type=F32)


def _dot_tn(a, b):
    return lax.dot_general(a, b, (((0,), (0,)), ((), ())), preferred_element_type=F32)


def _split3(x):
    x1 = x.astype(BF16)
    r1 = x - x1.astype(F32)
    x2 = r1.astype(BF16)
    x3 = (r1 - x2.astype(F32)).astype(BF16)
    return x1, x2, x3


def _log_sigmoid(x):
    return jnp.minimum(x, 0.0) - jnp.log1p(jnp.exp(-jnp.abs(x)))


def _ffn_kernel(xp_ref, x_ref, xn_ref, gain_ref, wu_ref, wg_ref, cw_ref, cb_ref, wd_ref, fin_ref,
                o_ref, h_ref, g_ref, a_ref, *, tm, nf, tf, blocks_per_seq, final):
    i = pl.program_id(0)
    halo = BF16_SUBLANES
    pos = i % blocks_per_seq
    gain = gain_ref[...]
    hp = jnp.where(pos == 0, 0.0, _rms(xp_ref[...], gain))
    hn = jnp.where(pos == blocks_per_seq - 1, 0.0, _rms(xn_ref[...], gain))
    x = x_ref[...]
    h_ref[0:halo, :] = hp.astype(BF16)
    h_ref[halo:halo + tm, :] = _rms(x, gain).astype(BF16)
    h_ref[halo + tm:, :] = hn.astype(BF16)
    sqrt_half = np.float32(np.sqrt(0.5))
    for j in range(nf):
        s = j % FFN_G_SLABS
        cols = slice(j * tf, (j + 1) * tf)
        g_ref[s] = _dot(h_ref[...], wg_ref[:, cols])
        cw = cw_ref[:, cols]
        gc = (g_ref[s, halo - 1:halo - 1 + tm, :] * cw[0:1, :] + g_ref[s, halo:halo + tm, :] * cw[1:2, :]
              + g_ref[s, halo + 1:halo + 1 + tm, :] * cw[2:3, :] + cb_ref[:, cols])
        u = _dot(h_ref[halo:halo + tm, :], wu_ref[:, cols])
        act = 0.5 * gc * (1.0 + lax.erf(gc * sqrt_half))
        a_ref[:, cols] = (act * u).astype(BF16)
    y = x + _dot(a_ref[...], wd_ref[...])
    o_ref[...] = _rms(y, fin_ref[...]) if final else y


def _ffn(x, seq_len, gain, wu, wg, cw, cb, wd, fin_gain, final):
    t, d = x.shape
    tm = min(FFN_ROW_TILE, seq_len)
    tf = FFN_COL_TILE
    nf = wu.shape[1] // tf
    halo = BF16_SUBLANES
    nblk = t // tm
    hb = tm // halo
    fixed = lambda i: (0, 0)
    kern = functools.partial(_ffn_kernel, tm=tm, nf=nf, tf=tf, blocks_per_seq=seq_len // tm, final=final)
    return pl.pallas_call(
        kern,
        grid=(nblk,),
        in_specs=[
            pl.BlockSpec((halo, d), lambda i: (jnp.maximum(i * hb - 1, 0), 0)),
            pl.BlockSpec((tm, d), lambda i: (i, 0)),
            pl.BlockSpec((halo, d), lambda i: (jnp.minimum((i + 1) * hb, nblk * hb - 1), 0)),
            pl.BlockSpec((1, d), lambda i: (0, 0)),
            pl.BlockSpec(wu.shape, fixed, pipeline_mode=pl.Buffered(1)),
            pl.BlockSpec(wg.shape, fixed, pipeline_mode=pl.Buffered(1)),
            pl.BlockSpec(cw.shape, fixed),
            pl.BlockSpec(cb.shape, fixed),
            pl.BlockSpec(wd.shape, fixed, pipeline_mode=pl.Buffered(1)),
            pl.BlockSpec((1, d), lambda i: (0, 0)),
        ],
        out_specs=pl.BlockSpec((tm, d), lambda i: (i, 0)),
        out_shape=jax.ShapeDtypeStruct((t, d), F32),
        scratch_shapes=[
            pltpu.VMEM((tm + 2 * halo, d), BF16),
            pltpu.VMEM((FFN_G_SLABS, tm + 2 * halo, tf), F32),
            pltpu.VMEM((tm, nf * tf), BF16),
        ],
        compiler_params=_cparams(1),
        name="conv_ffn",
    )(x, x, x, gain, wu, wg, cw, cb, wd, fin_gain)


def _ret_proj_kernel(x_ref, gain_ref, w_ref, cos_ref, sin_ref, q_ref, kt_ref, v_ref, g_ref):
    h = _rms(x_ref[...], gain_ref[...]).astype(BF16)
    c = cos_ref[...]
    s = sin_ref[...]
    half = RET_QK_DIM // 2
    for which in (0, 1):
        for hd in range(RET_HEADS):
            lo = hd * RET_QK_DIM
            acc = _dot(h, w_ref[:, which * RET_QK_WIDTH + lo:which * RET_QK_WIDTH + lo + RET_QK_DIM])
            x1 = acc[:, :half]
            x2 = acc[:, half:]
            r1 = x1 * c - x2 * s
            r2 = x2 * c + x1 * s
            if which == 0:
                q_ref[:, lo:lo + half] = r1.astype(BF16)
                q_ref[:, lo + half:lo + 2 * half] = r2.astype(BF16)
            else:
                scale = np.float32(RET_QK_DIM ** -0.5)
                kt_ref[0, lo:lo + half, :] = (r1 * scale).T.astype(BF16)
                kt_ref[0, lo + half:lo + 2 * half, :] = (r2 * scale).T.astype(BF16)
    v0 = 2 * RET_QK_WIDTH
    g0 = v0 + RET_V_WIDTH
    for hd in range(RET_HEADS):
        sl = slice(hd * RET_V_DIM, (hd + 1) * RET_V_DIM)
        v_ref[:, sl] = _dot(h, w_ref[:, v0 + hd * RET_V_DIM:v0 + (hd + 1) * RET_V_DIM]).astype(BF16)
        g_ref[:, sl] = _dot(h, w_ref[:, g0 + hd * RET_V_DIM:g0 + (hd + 1) * RET_V_DIM]).astype(BF16)


def _ret_proj(x, batch, seq_len, gain, w, cos, sin):
    t, d = x.shape
    tm = min(ROW_TILE, seq_len)
    sblk = seq_len // tm
    row = lambda i: (i, 0)
    return pl.pallas_call(
        _ret_proj_kernel,
        grid=(t // tm,),
        in_specs=[
            pl.BlockSpec((tm, d), row),
            pl.BlockSpec((1, d), lambda i: (0, 0)),
            pl.BlockSpec(w.shape, lambda i: (0, 0), pipeline_mode=pl.Buffered(1)),
            pl.BlockSpec((tm, LANES), lambda i: (i % sblk, 0)),
            pl.BlockSpec((tm, LANES), lambda i: (i % sblk, 0)),
        ],
        out_specs=[
            pl.BlockSpec((tm, RET_QK_WIDTH), row),
            pl.BlockSpec((1, RET_QK_WIDTH, tm), lambda i: (i // sblk, 0, i % sblk)),
            pl.BlockSpec((tm, RET_V_WIDTH), row),
            pl.BlockSpec((tm, RET_V_WIDTH), row),
        ],
        out_shape=[
            jax.ShapeDtypeStruct((t, RET_QK_WIDTH), BF16),
            jax.ShapeDtypeStruct((batch, RET_QK_WIDTH, seq_len), BF16),
            jax.ShapeDtypeStruct((t, RET_V_WIDTH), BF16),
            jax.ShapeDtypeStruct((t, RET_V_WIDTH), BF16),
        ],
        compiler_params=_cparams(1),
        name="ret_proj",
    )(x, gain, w, cos, sin)


def _ret_kernel(qf_ref, ktf_ref, vf_ref, qb_ref, ktb_ref, vb_ref, intra_ref, qd_ref, kd_ref, cd_ref,
                yf_ref, yb_ref, s_ref):
    c = pl.program_id(1)

    @pl.when(c == 0)
    def _():
        s_ref[...] = jnp.zeros_like(s_ref)

    dirs = ((qf_ref, ktf_ref, vf_ref, yf_ref), (qb_ref, ktb_ref, vb_ref, yb_ref))
    for d, (q_ref, kt_ref, v_ref, y_ref) in enumerate(dirs):
        for hd in range(RET_HEADS):
            idx = d * RET_HEADS + hd
            q = q_ref[:, hd * RET_QK_DIM:(hd + 1) * RET_QK_DIM]
            kt = kt_ref[0, hd * RET_QK_DIM:(hd + 1) * RET_QK_DIM, :]
            v = v_ref[:, hd * RET_V_DIM:(hd + 1) * RET_V_DIM]
            state = s_ref[idx]
            scores = _dot(q, kt) * intra_ref[d, hd]
            inter = _dot(q, state.astype(BF16)) * qd_ref[d, hd][:, 0:1]
            y_ref[:, hd * RET_V_DIM:(hd + 1) * RET_V_DIM] = (_dot(scores.astype(BF16), v) + inter).astype(BF16)
            kdec_t = (kt.astype(F32) * kd_ref[d, hd][0:1, :]).astype(BF16)
            s_ref[idx] = cd_ref[d, hd][0:1, 0:1] * state + _dot(kdec_t, v)


def _ret_core(q, kt, v, batch, seq_len, intra, qd, kd, cd):
    t = q.shape[0]
    chunk = intra.shape[-1]
    nc = seq_len // chunk
    fwd = lambda b, c: (b * nc + c, 0)
    bwd = lambda b, c: (b * nc + nc - 1 - c, 0)
    fwd_t = lambda b, c: (b, 0, c)
    bwd_t = lambda b, c: (b, 0, nc - 1 - c)
    const4 = lambda b, c: (0, 0, 0, 0)
    return pl.pallas_call(
        _ret_kernel,
        grid=(batch, nc),
        in_specs=[
            pl.BlockSpec((chunk, RET_QK_WIDTH), fwd),
            pl.BlockSpec((1, RET_QK_WIDTH, chunk), fwd_t),
            pl.BlockSpec((chunk, RET_V_WIDTH), fwd),
            pl.BlockSpec((chunk, RET_QK_WIDTH), bwd),
            pl.BlockSpec((1, RET_QK_WIDTH, chunk), bwd_t),
            pl.BlockSpec((chunk, RET_V_WIDTH), bwd),
            pl.BlockSpec(intra.shape, const4),
            pl.BlockSpec(qd.shape, const4),
            pl.BlockSpec(kd.shape, const4),
            pl.BlockSpec(cd.shape, const4),
        ],
        out_specs=[
            pl.BlockSpec((chunk, RET_V_WIDTH), fwd),
            pl.BlockSpec((chunk, RET_V_WIDTH), bwd),
        ],
        out_shape=[jax.ShapeDtypeStruct((t, RET_V_WIDTH), BF16)] * 2,
        scratch_shapes=[pltpu.VMEM((2 * RET_HEADS, RET_QK_DIM, RET_V_DIM), F32)],
        compiler_params=_cparams(2),
        name="retention",
    )(q, kt, v, q, kt, v, intra, qd, kd, cd)


def _ret_out_kernel(yf_ref, yb_ref, g_ref, gain_ref, w_ref, x_ref, o_ref):
    acc = x_ref[...]
    for hd in range(RET_HEADS):
        sl = slice(hd * RET_V_DIM, (hd + 1) * RET_V_DIM)
        y = _rms(yf_ref[:, sl].astype(F32) + yb_ref[:, sl].astype(F32), gain_ref[:, sl])
        g = g_ref[:, sl].astype(F32)
        z = (y * (g * jax.nn.sigmoid(g))).astype(BF16)
        acc = acc + _dot(z, w_ref[sl, :])
    o_ref[...] = acc


def _ret_out(yf, yb, g, gain, w, x, seq_len):
    t, d = x.shape
    tm = min(ROW_TILE, seq_len)
    row = lambda i: (i, 0)
    fixed = lambda i: (0, 0)
    return pl.pallas_call(
        _ret_out_kernel,
        grid=(t // tm,),
        in_specs=[
            pl.BlockSpec((tm, RET_V_WIDTH), row),
            pl.BlockSpec((tm, RET_V_WIDTH), row),
            pl.BlockSpec((tm, RET_V_WIDTH), row),
            pl.BlockSpec((1, RET_V_WIDTH), fixed),
            pl.BlockSpec((RET_V_WIDTH, d), fixed),
            pl.BlockSpec((tm, d), row),
        ],
        out_specs=pl.BlockSpec((tm, d), row),
        out_shape=jax.ShapeDtypeStruct((t, d), F32),
        compiler_params=_cparams(1),
        name="ret_out",
    )(yf, yb, g, gain, w, x)


def _ret_tables(decay_logit, chunk):
    log_gamma = jax.nn.log_sigmoid(decay_logit.astype(F32))
    pos = jnp.arange(chunk, dtype=F32)
    diff = pos[:, None] - pos[None, :]
    lg = log_gamma[:, :, None, None]
    tri = jnp.exp(jnp.maximum(diff, 0.0)[None, None] * lg)
    intra_f = jnp.where((diff >= 0)[None], tri[0], 0.0)
    intra_b = jnp.where((diff > 0)[None], tri[1], 0.0)
    intra_b = intra_b[:, ::-1, ::-1]
    qdec = jnp.exp((pos + 1.0)[None, None, :] * log_gamma[:, :, None])
    kdec = jnp.exp((chunk - 1.0 - pos)[None, None, :] * log_gamma[:, :, None])
    qdec = jnp.stack([qdec[0], qdec[1, :, ::-1]])
    kdec = jnp.stack([kdec[0], kdec[1, :, ::-1]])
    cdec = jnp.exp(chunk * log_gamma)
    rep = lambda a: jnp.broadcast_to(a[..., None], a.shape + (LANES,))
    return (jnp.stack([intra_f, intra_b]), rep(qdec), jnp.broadcast_to(kdec[:, :, None, :], kdec.shape[:2] + (8, chunk)),
            jnp.broadcast_to(cdec[:, :, None, None], cdec.shape + (8, LANES)))


def _ab_proj_kernel(x_ref, gain_ref, w_ref, wm_ref, cos_ref, sin_ref, qg_ref, kg_ref,
                    q_ref, kt_ref, v_ref, g_ref, gt_ref, mqv_ref, mkt_ref, mo_ref, *, tm):
    h = _rms(x_ref[...], gain_ref[...]).astype(BF16)
    lane = lax.broadcasted_iota(jnp.int32, (tm, LANES), 1)
    ri = lax.broadcasted_iota(jnp.int32, (LANES, LANES), 0)
    ci = lax.broadcasted_iota(jnp.int32, (LANES, LANES), 1)
    same_head = ((ri // ATTN_HEAD_DIM) == (ci // ATTN_HEAD_DIM)).astype(BF16)

    def head_sumsq(x):
        x2 = x * x
        hi = x2.astype(BF16)
        lo = (x2 - hi.astype(F32)).astype(BF16)
        return _dot(hi, same_head) + _dot(lo, same_head)

    pa = _dot(h, w_ref[...])
    nqk = (ATTN_Q_WIDTH + ATTN_KV_WIDTH) // LANES
    sumsq = [head_sumsq(pa[:, p * LANES:(p + 1) * LANES]) for p in range(nqk)]
    n = MLSTM_WIDTH
    mqv_ref[:, 0:n] = _dot(h, wm_ref[:, 0:n]).astype(BF16)
    mkt_ref[0] = (_dot(h, wm_ref[:, n:2 * n]) * np.float32(MLSTM_HEAD_DIM ** -0.5)).T.astype(BF16)
    mqv_ref[:, n:2 * n] = _dot(h, wm_ref[:, 2 * n:3 * n]).astype(BF16)
    mo_ref[...] = _dot(h, wm_ref[:, 3 * n:4 * n]).astype(BF16)
    gates = pa[:, ATTN_Q_WIDTH + 2 * LANES:]
    g_ref[...] = gates
    gt_ref[0] = gates.T[0:MLSTM_GATES, :]
    c = cos_ref[...]
    s = sin_ref[...]
    first_half = (lane % ATTN_HEAD_DIM) < (ATTN_HEAD_DIM // 2)
    low = lane < ATTN_HEAD_DIM

    def norm_rope(p, gain):
        x = pa[:, p * LANES:(p + 1) * LANES]
        xn = x * lax.rsqrt(sumsq[p] * np.float32(1.0 / ATTN_HEAD_DIM) + NORM_EPS) * gain
        swapped = jnp.where(first_half, pltpu.roll(xn, LANES - ATTN_HEAD_DIM // 2, 1),
                            pltpu.roll(xn, ATTN_HEAD_DIM // 2, 1))
        return xn * c + swapped * s

    for p in range(ATTN_HEADS // 2):
        y = norm_rope(p, qg_ref[...]) * np.float32(ATTN_HEAD_DIM ** -0.5 * np.log2(np.e))
        yr = pltpu.roll(y, ATTN_HEAD_DIM, 1)
        if (2 * p) // (ATTN_HEADS // ATTN_KV_HEADS) == 0:
            head_a, head_b = jnp.where(low, y, 0.0), jnp.where(low, yr, 0.0)
        else:
            head_a, head_b = jnp.where(low, 0.0, yr), jnp.where(low, 0.0, y)
        q_ref[0, 2 * p] = head_a.astype(BF16)
        q_ref[0, 2 * p + 1] = head_b.astype(BF16)
    k = norm_rope(nqk - 1, kg_ref[...])
    kt_ref[0] = k.T.astype(BF16)
    v_ref[0] = pa[:, ATTN_Q_WIDTH + LANES:ATTN_Q_WIDTH + 2 * LANES].astype(BF16)


def _ab_proj(x, batch, seq_len, gain, w, wm, cos, sin, qg, kg):
    t, d = x.shape
    tm = min(ROW_TILE, seq_len)
    sblk = seq_len // tm
    rows = lambda b, s: (b * sblk + s, 0)
    return pl.pallas_call(
        functools.partial(_ab_proj_kernel, tm=tm),
        grid=(batch, sblk),
        in_specs=[
            pl.BlockSpec((tm, d), rows),
            pl.BlockSpec((1, d), lambda b, s: (0, 0)),
            pl.BlockSpec(w.shape, lambda b, s: (0, 0), pipeline_mode=pl.Buffered(1)),
            pl.BlockSpec(wm.shape, lambda b, s: (0, 0), pipeline_mode=pl.Buffered(1)),
            pl.BlockSpec((tm, LANES), lambda b, s: (s, 0)),
            pl.BlockSpec((tm, LANES), lambda b, s: (s, 0)),
            pl.BlockSpec((1, LANES), lambda b, s: (0, 0)),
            pl.BlockSpec((1, LANES), lambda b, s: (0, 0)),
        ],
        out_specs=[
            pl.BlockSpec((1, ATTN_HEADS, tm, LANES), lambda b, s: (b, 0, s, 0)),
            pl.BlockSpec((1, LANES, tm), lambda b, s: (b, 0, s)),
            pl.BlockSpec((1, tm, LANES), lambda b, s: (b, s, 0)),
            pl.BlockSpec((tm, LANES), rows),
            pl.BlockSpec((1, MLSTM_GATES, tm), lambda b, s: (b, 0, s)),
            pl.BlockSpec((tm, 2 * MLSTM_WIDTH), rows),
            pl.BlockSpec((1, MLSTM_WIDTH, tm), lambda b, s: (b, 0, s)),
            pl.BlockSpec((tm, MLSTM_WIDTH), rows),
        ],
        out_shape=[
            jax.ShapeDtypeStruct((batch, ATTN_HEADS, seq_len, LANES), BF16),
            jax.ShapeDtypeStruct((batch, LANES, seq_len), BF16),
            jax.ShapeDtypeStruct((batch, seq_len, LANES), BF16),
            jax.ShapeDtypeStruct((t, LANES), F32),
            jax.ShapeDtypeStruct((batch, MLSTM_GATES, seq_len), F32),
            jax.ShapeDtypeStruct((t, 2 * MLSTM_WIDTH), BF16),
            jax.ShapeDtypeStruct((batch, MLSTM_WIDTH, seq_len), BF16),
            jax.ShapeDtypeStruct((t, MLSTM_WIDTH), BF16),
        ],
        compiler_params=_cparams(2),
        name="ab_proj",
    )(x, gain, w, wm, cos, sin, qg, kg)


def _attn_kernel(q_ref, kt_ref, v_ref, o_ref, *, tq, bounds):
    rows = ATTN_HEADS * tq
    q = q_ref[0].reshape(rows, LANES)
    m = jnp.full((rows, 1), -jnp.inf, F32)
    l = jnp.zeros((rows, 1), F32)
    acc = jnp.zeros((rows, LANES), F32)
    for lo, hi in zip(bounds[:-1], bounds[1:]):
        s = _dot(q, kt_ref[0, :, lo:hi])
        m_new = jnp.maximum(m, jnp.max(s, axis=-1, keepdims=True))
        alpha = jnp.exp2(m - m_new)
        p = jnp.exp2(s - m_new)
        l = alpha * l + jnp.sum(p, axis=-1, keepdims=True)
        acc = alpha * acc + _dot(p.astype(BF16), v_ref[0, lo:hi, :])
        m = m_new
    o = acc / l
    lane = lax.broadcasted_iota(jnp.int32, (tq, LANES), 1)
    low = lane < ATTN_HEAD_DIM
    for p2 in range(ATTN_HEADS // 2):
        a = o[(2 * p2) * tq:(2 * p2 + 1) * tq]
        b = o[(2 * p2 + 1) * tq:(2 * p2 + 2) * tq]
        if (2 * p2) // (ATTN_HEADS // ATTN_KV_HEADS) == 0:
            pair = jnp.where(low, a, pltpu.roll(b, ATTN_HEAD_DIM, 1))
        else:
            pair = jnp.where(low, pltpu.roll(a, ATTN_HEAD_DIM, 1), b)
        o_ref[0, :, p2 * LANES:(p2 + 1) * LANES] = pair.astype(BF16)


def _attention(q, kt, v, batch, seq_len):
    tq = ATTN_Q_TILE
    while tq > CHUNK and (ATTN_HEADS * tq * seq_len * 4 > ATTN_SCORE_BYTES or seq_len % tq):
        tq //= 2
    tk = min(ATTN_K_TILE, max(seq_len // 2, CHUNK))
    return pl.pallas_call(
        functools.partial(_attn_kernel, tq=tq, bounds=tuple(range(0, seq_len + 1, tk))),
        grid=(batch, seq_len // tq),
        in_specs=[
            pl.BlockSpec((1, ATTN_HEADS, tq, LANES), lambda b, i: (b, 0, i, 0)),
            pl.BlockSpec((1, LANES, seq_len), lambda b, i: (b, 0, 0)),
            pl.BlockSpec((1, seq_len, LANES), lambda b, i: (b, 0, 0)),
        ],
        out_specs=pl.BlockSpec((1, tq, ATTN_Q_WIDTH), lambda b, i: (b, i, 0)),
        out_shape=jax.ShapeDtypeStruct((batch, seq_len, ATTN_Q_WIDTH), BF16),
        compiler_params=_cparams(2),
        name="attention",
    )(q, kt, v)


def _dot3(x, m01, x_is_lhs=True):
    parts = _split3(x)
    if x_is_lhs:
        return _dot(parts[0], m01) + _dot(parts[1], m01) + _dot(parts[2], m01)
    return _dot(m01, parts[0]) + _dot(m01, parts[1]) + _dot(m01, parts[2])


def _mlstm_kernel(qvf_ref, ktf_ref, gf_ref, gtf_ref, qvb_ref, ktb_ref, gb_ref, gtb_ref, bias_ref, bias_t_ref,
                  hf_ref, hb_ref, cn_ref, m_ref, *, bb):
    c = pl.program_id(1)

    @pl.when(c == 0)
    def _():
        cn_ref[...] = jnp.zeros_like(cn_ref)
        m_ref[...] = jnp.zeros_like(m_ref)

    L = CHUNK
    dh = MLSTM_HEAD_DIM
    row = lax.broadcasted_iota(jnp.int32, (L, L), 0)
    col = lax.broadcasted_iota(jnp.int32, (L, L), 1)
    ones = jnp.ones((L, LANES), BF16)

    def chain(bi, d, qv_ref, kt_ref, g_ref, gt_ref, h_ref):
        allowed = (col <= row) if d == 0 else (col >= row)
        gates = g_ref[bi] + bias_ref[...]
        gates_t = gt_ref[bi] + bias_t_ref[...]
        logf = _log_sigmoid(gates)
        logf_t = _log_sigmoid(gates_t)
        sel_col = allowed.astype(BF16)
        sel_row = ((row <= col) if d == 0 else (row >= col)).astype(BF16)
        bcum = _dot3(logf, sel_col, x_is_lhs=False)
        bcum_t = _dot3(logf_t, sel_row)
        last = L - 1 if d == 0 else 0
        for hd in range(MLSTM_HEADS):
            idx = (bi * 2 + d) * MLSTM_HEADS + hd
            ig = 2 * MLSTM_HEADS * d + hd
            fg = ig + MLSTM_HEADS
            q = qv_ref[bi, :, hd * dh:(hd + 1) * dh]
            v = qv_ref[bi, :, MLSTM_WIDTH + hd * dh:MLSTM_WIDTH + (hd + 1) * dh]
            kt = kt_ref[bi, hd * dh:(hd + 1) * dh, :]
            v_ones = jnp.concatenate([v, ones], axis=1)
            a_row = gates_t[ig:ig + 1, :] - bcum_t[fg:fg + 1, :]
            b_last = jnp.broadcast_to(bcum_t[fg:fg + 1, last:last + 1], (1, LANES))
            b_col = jnp.broadcast_to(bcum[:, fg:fg + 1], (L, LANES))
            m_old = m_ref[idx]
            cn_old = cn_ref[idx]
            dlog = jnp.where(allowed, b_col + a_row, -jnp.inf)
            inter = b_col + m_old
            m_row = jnp.maximum(inter, jnp.max(dlog, axis=-1, keepdims=True))
            w = jnp.exp(dlog - m_row) * _dot(q, kt)
            inter_w = jnp.exp(inter - m_row)
            intra = _dot(w.astype(BF16), v_ones)
            carry = _dot(q, cn_old.astype(BF16))
            num = intra[:, :dh] + inter_w * carry[:, :dh]
            den = intra[:, dh:] + inter_w * carry[:, dh:]
            hval = num / jnp.maximum(jnp.abs(den), jnp.exp(-m_row))
            h_ref[bi, :, hd * dh:(hd + 1) * dh] = hval.astype(BF16)
            m_new = jnp.maximum(b_last + m_old, jnp.max(b_last[:, 0:1] + a_row, axis=-1, keepdims=True))
            keep = jnp.exp(b_last + m_old - m_new)
            wk_row = jnp.exp(b_last[:, 0:1] + a_row - m_new[:, 0:1])
            kw_t = (kt.astype(F32) * wk_row).astype(BF16)
            cn_ref[idx] = jnp.concatenate([keep, keep], axis=1) * cn_old + _dot(kw_t, v_ones)
            m_ref[idx] = m_new

    for bi in range(bb):
        chain(bi, 0, qvf_ref, ktf_ref, gf_ref, gtf_ref, hf_ref)
        chain(bi, 1, qvb_ref, ktb_ref, gb_ref, gtb_ref, hb_ref)


def _mlstm(qv, kt, gates, gt, bias, bias_t, batch, seq_len):
    nc = seq_len // CHUNK
    bb = min(MLSTM_BATCH_BLOCK, batch)
    qv = qv.reshape(batch, seq_len, qv.shape[-1])
    gates = gates.reshape(batch, seq_len, LANES)
    fwd = lambda b, c: (b, c, 0)
    bwd = lambda b, c: (b, nc - 1 - c, 0)
    fwd_t = lambda b, c: (b, 0, c)
    bwd_t = lambda b, c: (b, 0, nc - 1 - c)
    nstate = bb * 2 * MLSTM_HEADS
    hf, hb = pl.pallas_call(
        functools.partial(_mlstm_kernel, bb=bb),
        grid=(batch // bb, nc),
        in_specs=[
            pl.BlockSpec((bb, CHUNK, 2 * MLSTM_WIDTH), fwd),
            pl.BlockSpec((bb, MLSTM_WIDTH, CHUNK), fwd_t),
            pl.BlockSpec((bb, CHUNK, LANES), fwd),
            pl.BlockSpec((bb, MLSTM_GATES, CHUNK), fwd_t),
            pl.BlockSpec((bb, CHUNK, 2 * MLSTM_WIDTH), bwd),
            pl.BlockSpec((bb, MLSTM_WIDTH, CHUNK), bwd_t),
            pl.BlockSpec((bb, CHUNK, LANES), bwd),
            pl.BlockSpec((bb, MLSTM_GATES, CHUNK), bwd_t),
            pl.BlockSpec((1, LANES), lambda b, c: (0, 0)),
            pl.BlockSpec((MLSTM_GATES, LANES), lambda b, c: (0, 0)),
        ],
        out_specs=[
            pl.BlockSpec((bb, CHUNK, MLSTM_WIDTH), fwd),
            pl.BlockSpec((bb, CHUNK, MLSTM_WIDTH), bwd),
        ],
        out_shape=[jax.ShapeDtypeStruct((batch, seq_len, MLSTM_WIDTH), BF16)] * 2,
        scratch_shapes=[
            pltpu.VMEM((nstate, MLSTM_HEAD_DIM, 2 * MLSTM_HEAD_DIM), F32),
            pltpu.VMEM((nstate, 1, LANES), F32),
        ],
        compiler_params=_cparams(2),
        name="mlstm",
    )(qv, kt, gates, gt, qv, kt, gates, gt, bias, bias_t)
    return hf.reshape(batch * seq_len, MLSTM_WIDTH), hb.reshape(batch * seq_len, MLSTM_WIDTH)


def _ab_out_kernel(a_ref, hf_ref, hb_ref, mo_ref, gain_ref, w_ref, x_ref, o_ref):
    acc = x_ref[...] + _dot(a_ref[...], w_ref[0:ATTN_Q_WIDTH, :])
    for hd in range(MLSTM_HEADS):
        sl = slice(hd * MLSTM_HEAD_DIM, (hd + 1) * MLSTM_HEAD_DIM)
        y = _rms(hf_ref[:, sl].astype(F32) + hb_ref[:, sl].astype(F32), gain_ref[:, sl])
        z = (y * jax.nn.sigmoid(mo_ref[:, sl].astype(F32))).astype(BF16)
        acc = acc + _dot(z, w_ref[ATTN_Q_WIDTH + hd * MLSTM_HEAD_DIM:ATTN_Q_WIDTH + (hd + 1) * MLSTM_HEAD_DIM, :])
    o_ref[...] = acc


def _ab_out(attn, hf, hb, mo, gain, w, x, seq_len):
    t, d = x.shape
    tm = min(ROW_TILE, seq_len)
    row = lambda i: (i, 0)
    fixed = lambda i: (0, 0)
    return pl.pallas_call(
        _ab_out_kernel,
        grid=(t // tm,),
        in_specs=[
            pl.BlockSpec((tm, ATTN_Q_WIDTH), row),
            pl.BlockSpec((tm, MLSTM_WIDTH), row),
            pl.BlockSpec((tm, MLSTM_WIDTH), row),
            pl.BlockSpec((tm, MLSTM_WIDTH), row),
            pl.BlockSpec((1, MLSTM_WIDTH), fixed),
            pl.BlockSpec(w.shape, fixed),
            pl.BlockSpec((tm, d), row),
        ],
        out_specs=pl.BlockSpec((tm, d), row),
        out_shape=jax.ShapeDtypeStruct((t, d), F32),
        compiler_params=_cparams(1),
        name="ab_out",
    )(attn, hf, hb, mo, gain, w, x)


def _rope_angles(seq_len, head_dim):
    rows = seq_len // GRID_W
    row_idx = jnp.repeat(jnp.arange(rows, dtype=F32), GRID_W)
    col_idx = jnp.tile(jnp.arange(GRID_W, dtype=F32), rows)
    axis_dim = head_dim // 2
    inv_freq = ROPE_THETA ** (-jnp.arange(0, axis_dim, 2, dtype=F32) / axis_dim)
    ang = jnp.concatenate([row_idx[:, None] * inv_freq, col_idx[:, None] * inv_freq], axis=-1)
    return jnp.cos(ang), jnp.sin(ang)


def _prepare(p):
    d = p["ab_w_in"].shape[1]
    n_ab = p["ab_w_in"].shape[0]
    a_cols = ATTN_Q_WIDTH + 2 * ATTN_KV_WIDTH
    m_lo = a_cols
    m_hi = a_cols + 4 * MLSTM_WIDTH
    w_in = p["ab_w_in"]
    pad = jnp.zeros((n_ab, d, LANES - MLSTM_GATES), w_in.dtype)
    out = dict(
        ab_w_attn=jnp.concatenate([w_in[:, :, :a_cols], w_in[:, :, m_hi:], pad], axis=-1).astype(BF16),
        ab_w_mlstm=w_in[:, :, m_lo:m_hi].astype(BF16),
        ab_bias=jnp.pad(p["ab_gate_bias"].astype(F32), ((0, 0), (0, LANES - MLSTM_GATES)))[:, None, :],
        ab_bias_t=jnp.broadcast_to(p["ab_gate_bias"].astype(F32)[:, :, None], (n_ab, MLSTM_GATES, LANES)),
        ab_w_out=p["ab_w_out"].astype(BF16),
        attn_qg=jnp.tile(p["attn_q_norm"].astype(F32), (1, LANES // ATTN_HEAD_DIM))[:, None, :],
        attn_kg=jnp.tile(p["attn_k_norm"].astype(F32), (1, LANES // ATTN_HEAD_DIM))[:, None, :],
        mlstm_gain=p["mlstm_out_norm"].astype(F32)[:, None, :],
        ret_w_in=p["ret_w_in"].astype(BF16),
        ret_w_out=p["ret_w_out"].astype(BF16),
        ret_gain=p["ret_out_norm"].astype(F32)[:, None, :],
        ret_decay_logit=p["ret_decay_logit"],
        norm_mix=p["norm_mix"].astype(F32)[:, None, :],
        norm_ffn=p["norm_ffn"].astype(F32)[:, None, :],
        norm_final=p["norm_final"].astype(F32)[None, :],
    )
    depth, _, ff2 = p["ffn_w_up"].shape
    dff = ff2 // 2
    out["ffn_wu"] = p["ffn_w_up"][:, :, :dff].astype(BF16)
    out["ffn_wg"] = p["ffn_w_up"][:, :, dff:].astype(BF16)
    out["ffn_wd"] = p["ffn_w_down"].astype(BF16)
    out["ffn_cw"] = jnp.pad(p["ffn_conv_w"].astype(F32), ((0, 0), (0, 5), (0, 0)))
    out["ffn_cb"] = p["ffn_conv_b"].astype(F32)[:, None, :]
    return out


def _trunk(x3, w):
    batch, seq_len, d = x3.shape
    x = x3.reshape(batch * seq_len, d)
    cos_a, sin_a = _rope_angles(seq_len, ATTN_HEAD_DIM)
    reps = LANES // (ATTN_HEAD_DIM // 2)
    cos_a = jnp.tile(cos_a, (1, reps))
    sin_a = jnp.tile(jnp.concatenate([-sin_a, sin_a], axis=-1), (1, reps // 2))
    cos_r, sin_r = _rope_angles(seq_len, RET_QK_DIM)
    depth = w["norm_mix"].shape[0]
    for layer in range(depth):
        j = layer // 2
        if layer % 2 == 0:
            q, kt, v, gates, gt, qv, mkt, mo = _ab_proj(
                x, batch, seq_len, w["norm_mix"][layer], w["ab_w_attn"][j], w["ab_w_mlstm"][j],
                cos_a, sin_a, w["attn_qg"][j], w["attn_kg"][j])
            attn = _attention(q, kt, v, batch, seq_len).reshape(batch * seq_len, ATTN_Q_WIDTH)
            hf, hb = _mlstm(qv, mkt, gates, gt, w["ab_bias"][j], w["ab_bias_t"][j], batch, seq_len)
            x = _ab_out(attn, hf, hb, mo, w["mlstm_gain"][j], w["ab_w_out"][j], x, seq_len)
        else:
            rq, rkt, v, g = _ret_proj(x, batch, seq_len, w["norm_mix"][layer], w["ret_w_in"][j], cos_r, sin_r)
            tables = _ret_tables(w["ret_decay_logit"][j], min(RET_CHUNK, seq_len))
            yf, yb = _ret_core(rq, rkt, v, batch, seq_len, *tables)
            x = _ret_out(yf, yb, g, w["ret_gain"][j], w["ret_w_out"][j], x, seq_len)
        x = _ffn(x, seq_len, w["norm_ffn"][layer], w["ffn_wu"][layer], w["ffn_wg"][layer], w["ffn_cw"][layer],
                 w["ffn_cb"][layer], w["ffn_wd"][layer], w["norm_final"], final=(layer == depth - 1))
    return x.reshape(batch, seq_len, d)


def kernel(x_prompt, x_sample, norm_mix, norm_ffn, norm_final, ab_w_in, ab_gate_bias, attn_q_norm, attn_k_norm,
           mlstm_out_norm, ab_w_out, ret_w_in, ret_decay_logit, ret_out_norm, ret_w_out, ffn_w_up, ffn_conv_w,
           ffn_conv_b, ffn_w_down):
    w = _prepare(dict(
        norm_mix=norm_mix, norm_ffn=norm_ffn, norm_final=norm_final, ab_w_in=ab_w_in, ab_gate_bias=ab_gate_bias,
        attn_q_norm=attn_q_norm, attn_k_norm=attn_k_norm, mlstm_out_norm=mlstm_out_norm, ab_w_out=ab_w_out,
        ret_w_in=ret_w_in, ret_decay_logit=ret_decay_logit, ret_out_norm=ret_out_norm, ret_w_out=ret_w_out,
        ffn_w_up=ffn_w_up, ffn_conv_w=ffn_conv_w, ffn_conv_b=ffn_conv_b, ffn_w_down=ffn_w_down))
    return (_trunk(x_prompt, w), _trunk(x_sample, w))
```

```python
import functools

import numpy as np
import jax
import jax.numpy as jnp
from jax import lax
from jax.experimental import pallas as pl
from jax.experimental.pallas import tpu as pltpu

F32 = jnp.float32
BF16 = jnp.bfloat16

GRID_W = 64
ROPE_THETA = 10000.0
NORM_EPS = 1e-6
ATTN_HEADS = 8
ATTN_KV_HEADS = 2
ATTN_HEAD_DIM = 64
ATTN_Q_WIDTH = ATTN_HEADS * ATTN_HEAD_DIM
ATTN_KV_WIDTH = ATTN_KV_HEADS * ATTN_HEAD_DIM
MLSTM_HEADS = 4
MLSTM_HEAD_DIM = 128
MLSTM_WIDTH = MLSTM_HEADS * MLSTM_HEAD_DIM
MLSTM_GATES = 4 * MLSTM_HEADS
RET_HEADS = 4
RET_QK_DIM = 256
RET_V_DIM = 512
RET_QK_WIDTH = RET_HEADS * RET_QK_DIM
RET_V_WIDTH = RET_HEADS * RET_V_DIM
CHUNK = 128
MLSTM_CHUNK = 256
RET_CHUNK = 256

LANES = 128
BF16_SUBLANES = 16
VMEM_LIMIT_BYTES = 56 * 1024 * 1024

ROW_TILE = 512
FFN_ROW_TILE = 1024
FFN_COL_TILE = 256
FFN_G_SLABS = 3
ATTN_Q_TILE = 256
ATTN_K_TILE = 2048
ATTN_SCORE_BYTES = 32 * 1024 * 1024
MLSTM_BATCH_BLOCK = 4


def _cparams(n_axes):
    return pltpu.CompilerParams(
        dimension_semantics=("arbitrary",) * n_axes, vmem_limit_bytes=VMEM_LIMIT_BYTES)


def _rms(x, gain):
    ms = jnp.mean(x * x, axis=-1, keepdims=True)
    return x * lax.rsqrt(ms + NORM_EPS) * gain


def _dot(a, b):
    return jnp.dot(a, b, preferred_element_type=F32)


def _dot_nt(a, b):
    return lax.dot_general(a, b, (((1,), (1,)), ((), ())), preferred_element_type=F32)


def _dot_tn(a, b):
    return lax.dot_general(a, b, (((0,), (0,)), ((), ())), preferred_element_type=F32)


def _split3(x):
    x1 = x.astype(BF16)
    r1 = x - x1.astype(F32)
    x2 = r1.astype(BF16)
    x3 = (r1 - x2.astype(F32)).astype(BF16)
    return x1, x2, x3


def _log_sigmoid(x):
    return jnp.minimum(x, 0.0) - jnp.log1p(jnp.exp(-jnp.abs(x)))


def _ffn_kernel(xp_ref, x_ref, xn_ref, gain_ref, wu_ref, wg_ref, cw_ref, cb_ref, wd_ref, fin_ref,
                o_ref, h_ref, g_ref, a_ref, *, tm, nf, tf, blocks_per_seq, final):
    i = pl.program_id(0)
    halo = BF16_SUBLANES
    pos = i % blocks_per_seq
    gain = gain_ref[...]
    hp = jnp.where(pos == 0, 0.0, _rms(xp_ref[...], gain))
    hn = jnp.where(pos == blocks_per_seq - 1, 0.0, _rms(xn_ref[...], gain))
    x = x_ref[...]
    h_ref[0:halo, :] = hp.astype(BF16)
    h_ref[halo:halo + tm, :] = _rms(x, gain).astype(BF16)
    h_ref[halo + tm:, :] = hn.astype(BF16)
    sqrt_half = np.float32(np.sqrt(0.5))
    for j in range(nf):
        s = j % FFN_G_SLABS
        cols = slice(j * tf, (j + 1) * tf)
        g_ref[s] = _dot(h_ref[...], wg_ref[:, cols])
        cw = cw_ref[:, cols]
        gc = (g_ref[s, halo - 1:halo - 1 + tm, :] * cw[0:1, :] + g_ref[s, halo:halo + tm, :] * cw[1:2, :]
              + g_ref[s, halo + 1:halo + 1 + tm, :] * cw[2:3, :] + cb_ref[:, cols])
        u = _dot(h_ref[halo:halo + tm, :], wu_ref[:, cols])
        act = 0.5 * gc * (1.0 + lax.erf(gc * sqrt_half))
        a_ref[:, cols] = (act * u).astype(BF16)
    y = x + _dot(a_ref[...], wd_ref[...])
    o_ref[...] = _rms(y, fin_ref[...]) if final else y


def _ffn(x, seq_len, gain, wu, wg, cw, cb, wd, fin_gain, final):
    t, d = x.shape
    tm = min(FFN_ROW_TILE, seq_len)
    tf = FFN_COL_TILE
    nf = wu.shape[1] // tf
    halo = BF16_SUBLANES
    nblk = t // tm
    hb = tm // halo
    fixed = lambda i: (0, 0)
    kern = functools.partial(_ffn_kernel, tm=tm, nf=nf, tf=tf, blocks_per_seq=seq_len // tm, final=final)
    return pl.pallas_call(
        kern,
        grid=(nblk,),
        in_specs=[
            pl.BlockSpec((halo, d), lambda i: (jnp.maximum(i * hb - 1, 0), 0)),
            pl.BlockSpec((tm, d), lambda i: (i, 0)),
            pl.BlockSpec((halo, d), lambda i: (jnp.minimum((i + 1) * hb, nblk * hb - 1), 0)),
            pl.BlockSpec((1, d), lambda i: (0, 0)),
            pl.BlockSpec(wu.shape, fixed, pipeline_mode=pl.Buffered(1)),
            pl.BlockSpec(wg.shape, fixed, pipeline_mode=pl.Buffered(1)),
            pl.BlockSpec(cw.shape, fixed),
            pl.BlockSpec(cb.shape, fixed),
            pl.BlockSpec(wd.shape, fixed, pipeline_mode=pl.Buffered(1)),
            pl.BlockSpec((1, d), lambda i: (0, 0)),
        ],
        out_specs=pl.BlockSpec((tm, d), lambda i: (i, 0)),
        out_shape=jax.ShapeDtypeStruct((t, d), F32),
        scratch_shapes=[
            pltpu.VMEM((tm + 2 * halo, d), BF16),
            pltpu.VMEM((FFN_G_SLABS, tm + 2 * halo, tf), F32),
            pltpu.VMEM((tm, nf * tf), BF16),
        ],
        compiler_params=_cparams(1),
        name="conv_ffn",
    )(x, x, x, gain, wu, wg, cw, cb, wd, fin_gain)


def _ret_proj_kernel(x_ref, gain_ref, w_ref, cos_ref, sin_ref, q_ref, kt_ref, v_ref, g_ref):
    h = _rms(x_ref[...], gain_ref[...]).astype(BF16)
    c = cos_ref[...]
    s = sin_ref[...]
    half = RET_QK_DIM // 2
    for which in (0, 1):
        for hd in range(RET_HEADS):
            lo = hd * RET_QK_DIM
            acc = _dot(h, w_ref[:, which * RET_QK_WIDTH + lo:which * RET_QK_WIDTH + lo + RET_QK_DIM])
            x1 = acc[:, :half]
            x2 = acc[:, half:]
            r1 = x1 * c - x2 * s
            r2 = x2 * c + x1 * s
            if which == 0:
                q_ref[:, lo:lo + half] = r1.astype(BF16)
                q_ref[:, lo + half:lo + 2 * half] = r2.astype(BF16)
            else:
                scale = np.float32(RET_QK_DIM ** -0.5)
                kt_ref[0, lo:lo + half, :] = (r1 * scale).T.astype(BF16)
                kt_ref[0, lo + half:lo + 2 * half, :] = (r2 * scale).T.astype(BF16)
    v0 = 2 * RET_QK_WIDTH
    g0 = v0 + RET_V_WIDTH
    for hd in range(RET_HEADS):
        sl = slice(hd * RET_V_DIM, (hd + 1) * RET_V_DIM)
        v_ref[:, sl] = _dot(h, w_ref[:, v0 + hd * RET_V_DIM:v0 + (hd + 1) * RET_V_DIM]).astype(BF16)
        g_ref[:, sl] = _dot(h, w_ref[:, g0 + hd * RET_V_DIM:g0 + (hd + 1) * RET_V_DIM]).astype(BF16)


def _ret_proj(x, batch, seq_len, gain, w, cos, sin):
    t, d = x.shape
    tm = min(ROW_TILE, seq_len)
    sblk = seq_len // tm
    row = lambda i: (i, 0)
    return pl.pallas_call(
        _ret_proj_kernel,
        grid=(t // tm,),
        in_specs=[
            pl.BlockSpec((tm, d), row),
            pl.BlockSpec((1, d), lambda i: (0, 0)),
            pl.BlockSpec(w.shape, lambda i: (0, 0), pipeline_mode=pl.Buffered(1)),
            pl.BlockSpec((tm, LANES), lambda i: (i % sblk, 0)),
            pl.BlockSpec((tm, LANES), lambda i: (i % sblk, 0)),
        ],
        out_specs=[
            pl.BlockSpec((tm, RET_QK_WIDTH), row),
            pl.BlockSpec((1, RET_QK_WIDTH, tm), lambda i: (i // sblk, 0, i % sblk)),
            pl.BlockSpec((tm, RET_V_WIDTH), row),
            pl.BlockSpec((tm, RET_V_WIDTH), row),
        ],
        out_shape=[
            jax.ShapeDtypeStruct((t, RET_QK_WIDTH), BF16),
            jax.ShapeDtypeStruct((batch, RET_QK_WIDTH, seq_len), BF16),
            jax.ShapeDtypeStruct((t, RET_V_WIDTH), BF16),
            jax.ShapeDtypeStruct((t, RET_V_WIDTH), BF16),
        ],
        compiler_params=_cparams(1),
        name="ret_proj",
    )(x, gain, w, cos, sin)


def _ret_kernel(qf_ref, ktf_ref, vf_ref, qb_ref, ktb_ref, vb_ref, intra_ref, qd_ref, kd_ref, cd_ref,
                yf_ref, yb_ref, s_ref):
    c = pl.program_id(1)

    @pl.when(c == 0)
    def _():
        s_ref[...] = jnp.zeros_like(s_ref)

    dirs = ((qf_ref, ktf_ref, vf_ref, yf_ref), (qb_ref, ktb_ref, vb_ref, yb_ref))
    for d, (q_ref, kt_ref, v_ref, y_ref) in enumerate(dirs):
        for hd in range(RET_HEADS):
            idx = d * RET_HEADS + hd
            q = q_ref[:, hd * RET_QK_DIM:(hd + 1) * RET_QK_DIM]
            kt = kt_ref[0, hd * RET_QK_DIM:(hd + 1) * RET_QK_DIM, :]
            v = v_ref[:, hd * RET_V_DIM:(hd + 1) * RET_V_DIM]
            state = s_ref[idx]
            scores = _dot(q, kt) * intra_ref[d, hd]
            inter = _dot(q, state.astype(BF16)) * qd_ref[d, hd][:, 0:1]
            y_ref[:, hd * RET_V_DIM:(hd + 1) * RET_V_DIM] = (_dot(scores.astype(BF16), v) + inter).astype(BF16)
            kdec_t = (kt.astype(F32) * kd_ref[d, hd][0:1, :]).astype(BF16)
            s_ref[idx] = cd_ref[d, hd][0:1, 0:1] * state + _dot(kdec_t, v)


def _ret_core(q, kt, v, batch, seq_len, intra, qd, kd, cd):
    t = q.shape[0]
    chunk = intra.shape[-1]
    nc = seq_len // chunk
    fwd = lambda b, c: (b * nc + c, 0)
    bwd = lambda b, c: (b * nc + nc - 1 - c, 0)
    fwd_t = lambda b, c: (b, 0, c)
    bwd_t = lambda b, c: (b, 0, nc - 1 - c)
    const4 = lambda b, c: (0, 0, 0, 0)
    return pl.pallas_call(
        _ret_kernel,
        grid=(batch, nc),
        in_specs=[
            pl.BlockSpec((chunk, RET_QK_WIDTH), fwd),
            pl.BlockSpec((1, RET_QK_WIDTH, chunk), fwd_t),
            pl.BlockSpec((chunk, RET_V_WIDTH), fwd),
            pl.BlockSpec((chunk, RET_QK_WIDTH), bwd),
            pl.BlockSpec((1, RET_QK_WIDTH, chunk), bwd_t),
            pl.BlockSpec((chunk, RET_V_WIDTH), bwd),
            pl.BlockSpec(intra.shape, const4),
            pl.BlockSpec(qd.shape, const4),
            pl.BlockSpec(kd.shape, const4),
            pl.BlockSpec(cd.shape, const4),
        ],
        out_specs=[
            pl.BlockSpec((chunk, RET_V_WIDTH), fwd),
            pl.BlockSpec((chunk, RET_V_WIDTH), bwd),
        ],
        out_shape=[jax.ShapeDtypeStruct((t, RET_V_WIDTH), BF16)] * 2,
        scratch_shapes=[pltpu.VMEM((2 * RET_HEADS, RET_QK_DIM, RET_V_DIM), F32)],
        compiler_params=_cparams(2),
        name="retention",
    )(q, kt, v, q, kt, v, intra, qd, kd, cd)


def _ret_out_kernel(yf_ref, yb_ref, g_ref, gain_ref, w_ref, x_ref, o_ref):
    acc = x_ref[...]
    for hd in range(RET_HEADS):
        sl = slice(hd * RET_V_DIM, (hd + 1) * RET_V_DIM)
        y = _rms(yf_ref[:, sl].astype(F32) + yb_ref[:, sl].astype(F32), gain_ref[:, sl])
        g = g_ref[:, sl].astype(F32)
        z = (y * (g * jax.nn.sigmoid(g))).astype(BF16)
        acc = acc + _dot(z, w_ref[sl, :])
    o_ref[...] = acc


def _ret_out(yf, yb, g, gain, w, x, seq_len):
    t, d = x.shape
    tm = min(ROW_TILE, seq_len)
    row = lambda i: (i, 0)
    fixed = lambda i: (0, 0)
    return pl.pallas_call(
        _ret_out_kernel,
        grid=(t // tm,),
        in_specs=[
            pl.BlockSpec((tm, RET_V_WIDTH), row),
            pl.BlockSpec((tm, RET_V_WIDTH), row),
            pl.BlockSpec((tm, RET_V_WIDTH), row),
            pl.BlockSpec((1, RET_V_WIDTH), fixed),
            pl.BlockSpec((RET_V_WIDTH, d), fixed),
            pl.BlockSpec((tm, d), row),
        ],
        out_specs=pl.BlockSpec((tm, d), row),
        out_shape=jax.ShapeDtypeStruct((t, d), F32),
        compiler_params=_cparams(1),
        name="ret_out",
    )(yf, yb, g, gain, w, x)


def _ret_tables(decay_logit, chunk):
    log_gamma = jax.nn.log_sigmoid(decay_logit.astype(F32))
    pos = jnp.arange(chunk, dtype=F32)
    diff = pos[:, None] - pos[None, :]
    lg = log_gamma[:, :, None, None]
    tri = jnp.exp(jnp.maximum(diff, 0.0)[None, None] * lg)
    intra_f = jnp.where((diff >= 0)[None], tri[0], 0.0)
    intra_b = jnp.where((diff > 0)[None], tri[1], 0.0)
    intra_b = intra_b[:, ::-1, ::-1]
    qdec = jnp.exp((pos + 1.0)[None, None, :] * log_gamma[:, :, None])
    kdec = jnp.exp((chunk - 1.0 - pos)[None, None, :] * log_gamma[:, :, None])
    qdec = jnp.stack([qdec[0], qdec[1, :, ::-1]])
    kdec = jnp.stack([kdec[0], kdec[1, :, ::-1]])
    cdec = jnp.exp(chunk * log_gamma)
    rep = lambda a: jnp.broadcast_to(a[..., None], a.shape + (LANES,))
    return (jnp.stack([intra_f, intra_b]), rep(qdec), jnp.broadcast_to(kdec[:, :, None, :], kdec.shape[:2] + (8, chunk)),
            jnp.broadcast_to(cdec[:, :, None, None], cdec.shape + (8, LANES)))


def _ab_proj_kernel(x_ref, gain_ref, w_ref, wm_ref, cos_ref, sin_ref, qg_ref, kg_ref,
                    q_ref, kt_ref, v_ref, g_ref, gt_ref, mqv_ref, mkt_ref, mo_ref, *, tm):
    h = _rms(x_ref[...], gain_ref[...]).astype(BF16)
    lane = lax.broadcasted_iota(jnp.int32, (tm, LANES), 1)
    ri = lax.broadcasted_iota(jnp.int32, (LANES, LANES), 0)
    ci = lax.broadcasted_iota(jnp.int32, (LANES, LANES), 1)
    same_head = ((ri // ATTN_HEAD_DIM) == (ci // ATTN_HEAD_DIM)).astype(BF16)

    def head_sumsq(x):
        x2 = x * x
        hi = x2.astype(BF16)
        lo = (x2 - hi.astype(F32)).astype(BF16)
        return _dot(hi, same_head) + _dot(lo, same_head)

    pa = _dot(h, w_ref[...])
    nqk = (ATTN_Q_WIDTH + ATTN_KV_WIDTH) // LANES
    sumsq = [head_sumsq(pa[:, p * LANES:(p + 1) * LANES]) for p in range(nqk)]
    n = MLSTM_WIDTH
    mqv_ref[:, 0:n] = _dot(h, wm_ref[:, 0:n]).astype(BF16)
    mkt_ref[0] = (_dot(h, wm_ref[:, n:2 * n]) * np.float32(MLSTM_HEAD_DIM ** -0.5)).T.astype(BF16)
    mqv_ref[:, n:2 * n] = _dot(h, wm_ref[:, 2 * n:3 * n]).astype(BF16)
    mo_ref[...] = _dot(h, wm_ref[:, 3 * n:4 * n]).astype(BF16)
    gates = pa[:, ATTN_Q_WIDTH + 2 * LANES:]
    g_ref[...] = gates
    gt_ref[0] = gates.T[0:MLSTM_GATES, :]
    c = cos_ref[...]
    s = sin_ref[...]
    first_half = (lane % ATTN_HEAD_DIM) < (ATTN_HEAD_DIM // 2)
    low = lane < ATTN_HEAD_DIM

    def norm_rope(p, gain):
        x = pa[:, p * LANES:(p + 1) * LANES]
        xn = x * lax.rsqrt(sumsq[p] * np.float32(1.0 / ATTN_HEAD_DIM) + NORM_EPS) * gain
        swapped = jnp.where(first_half, pltpu.roll(xn, LANES - ATTN_HEAD_DIM // 2, 1),
                            pltpu.roll(xn, ATTN_HEAD_DIM // 2, 1))
        return xn * c + swapped * s

    for p in range(ATTN_HEADS // 2):
        y = norm_rope(p, qg_ref[...]) * np.float32(ATTN_HEAD_DIM ** -0.5 * np.log2(np.e))
        yr = pltpu.roll(y, ATTN_HEAD_DIM, 1)
        if (2 * p) // (ATTN_HEADS // ATTN_KV_HEADS) == 0:
            head_a, head_b = jnp.where(low, y, 0.0), jnp.where(low, yr, 0.0)
        else:
            head_a, head_b = jnp.where(low, 0.0, yr), jnp.where(low, 0.0, y)
        q_ref[0, 2 * p] = head_a.astype(BF16)
        q_ref[0, 2 * p + 1] = head_b.astype(BF16)
    k = norm_rope(nqk - 1, kg_ref[...])
    kt_ref[0] = k.T.astype(BF16)
    v_ref[0] = pa[:, ATTN_Q_WIDTH + LANES:ATTN_Q_WIDTH + 2 * LANES].astype(BF16)


def _ab_proj(x, batch, seq_len, gain, w, wm, cos, sin, qg, kg):
    t, d = x.shape
    tm = min(ROW_TILE, seq_len)
    sblk = seq_len // tm
    rows = lambda b, s: (b * sblk + s, 0)
    return pl.pallas_call(
        functools.partial(_ab_proj_kernel, tm=tm),
        grid=(batch, sblk),
        in_specs=[
            pl.BlockSpec((tm, d), rows),
            pl.BlockSpec((1, d), lambda b, s: (0, 0)),
            pl.BlockSpec(w.shape, lambda b, s: (0, 0), pipeline_mode=pl.Buffered(1)),
            pl.BlockSpec(wm.shape, lambda b, s: (0, 0), pipeline_mode=pl.Buffered(1)),
            pl.BlockSpec((tm, LANES), lambda b, s: (s, 0)),
            pl.BlockSpec((tm, LANES), lambda b, s: (s, 0)),
            pl.BlockSpec((1, LANES), lambda b, s: (0, 0)),
            pl.BlockSpec((1, LANES), lambda b, s: (0, 0)),
        ],
        out_specs=[
            pl.BlockSpec((1, ATTN_HEADS, tm, LANES), lambda b, s: (b, 0, s, 0)),
            pl.BlockSpec((1, LANES, tm), lambda b, s: (b, 0, s)),
            pl.BlockSpec((1, tm, LANES), lambda b, s: (b, s, 0)),
            pl.BlockSpec((tm, LANES), rows),
            pl.BlockSpec((1, MLSTM_GATES, tm), lambda b, s: (b, 0, s)),
            pl.BlockSpec((tm, 2 * MLSTM_WIDTH), rows),
            pl.BlockSpec((1, MLSTM_WIDTH, tm), lambda b, s: (b, 0, s)),
            pl.BlockSpec((tm, MLSTM_WIDTH), rows),
        ],
        out_shape=[
            jax.ShapeDtypeStruct((batch, ATTN_HEADS, seq_len, LANES), BF16),
            jax.ShapeDtypeStruct((batch, LANES, seq_len), BF16),
            jax.ShapeDtypeStruct((batch, seq_len, LANES), BF16),
            jax.ShapeDtypeStruct((t, LANES), F32),
            jax.ShapeDtypeStruct((batch, MLSTM_GATES, seq_len), F32),
            jax.ShapeDtypeStruct((t, 2 * MLSTM_WIDTH), BF16),
            jax.ShapeDtypeStruct((batch, MLSTM_WIDTH, seq_len), BF16),
            jax.ShapeDtypeStruct((t, MLSTM_WIDTH), BF16),
        ],
        compiler_params=_cparams(2),
        name="ab_proj",
    )(x, gain, w, wm, cos, sin, qg, kg)


def _attn_kernel(q_ref, kt_ref, v_ref, o_ref, *, tq, bounds):
    rows = ATTN_HEADS * tq
    q = q_ref[0].reshape(rows, LANES)
    m = jnp.full((rows, 1), -jnp.inf, F32)
    l = jnp.zeros((rows, 1), F32)
    acc = jnp.zeros((rows, LANES), F32)
    for lo, hi in zip(bounds[:-1], bounds[1:]):
        s = _dot(q, kt_ref[0, :, lo:hi])
        m_new = jnp.maximum(m, jnp.max(s, axis=-1, keepdims=True))
        alpha = jnp.exp2(m - m_new)
        p = jnp.exp2(s - m_new)
        l = alpha * l + jnp.sum(p, axis=-1, keepdims=True)
        acc = alpha * acc + _dot(p.astype(BF16), v_ref[0, lo:hi, :])
        m = m_new
    o = acc / l
    lane = lax.broadcasted_iota(jnp.int32, (tq, LANES), 1)
    low = lane < ATTN_HEAD_DIM
    for p2 in range(ATTN_HEADS // 2):
        a = o[(2 * p2) * tq:(2 * p2 + 1) * tq]
        b = o[(2 * p2 + 1) * tq:(2 * p2 + 2) * tq]
        if (2 * p2) // (ATTN_HEADS // ATTN_KV_HEADS) == 0:
            pair = jnp.where(low, a, pltpu.roll(b, ATTN_HEAD_DIM, 1))
        else:
            pair = jnp.where(low, pltpu.roll(a, ATTN_HEAD_DIM, 1), b)
        o_ref[0, :, p2 * LANES:(p2 + 1) * LANES] = pair.astype(BF16)


def _attention(q, kt, v, batch, seq_len):
    tq = ATTN_Q_TILE
    while tq > CHUNK and (ATTN_HEADS * tq * seq_len * 4 > ATTN_SCORE_BYTES or seq_len % tq):
        tq //= 2
    tk = min(ATTN_K_TILE, max(seq_len // 2, CHUNK))
    return pl.pallas_call(
        functools.partial(_attn_kernel, tq=tq, bounds=tuple(range(0, seq_len + 1, tk))),
        grid=(batch, seq_len // tq),
        in_specs=[
            pl.BlockSpec((1, ATTN_HEADS, tq, LANES), lambda b, i: (b, 0, i, 0)),
            pl.BlockSpec((1, LANES, seq_len), lambda b, i: (b, 0, 0)),
            pl.BlockSpec((1, seq_len, LANES), lambda b, i: (b, 0, 0)),
        ],
        out_specs=pl.BlockSpec((1, tq, ATTN_Q_WIDTH), lambda b, i: (b, i, 0)),
        out_shape=jax.ShapeDtypeStruct((batch, seq_len, ATTN_Q_WIDTH), BF16),
        compiler_params=_cparams(2),
        name="attention",
    )(q, kt, v)


def _dot3(x, m01, x_is_lhs=True):
    parts = _split3(x)
    if x_is_lhs:
        return _dot(parts[0], m01) + _dot(parts[1], m01) + _dot(parts[2], m01)
    return _dot(m01, parts[0]) + _dot(m01, parts[1]) + _dot(m01, parts[2])


def _mlstm_kernel(qvf_ref, ktf_ref, gf_ref, gtf_ref, qvb_ref, ktb_ref, gb_ref, gtb_ref, bias_ref, bias_t_ref,
                  hf_ref, hb_ref, cn_ref, m_ref, *, bb, L):
    c = pl.program_id(1)

    @pl.when(c == 0)
    def _():
        cn_ref[...] = jnp.zeros_like(cn_ref)
        m_ref[...] = jnp.zeros_like(m_ref)

    dh = MLSTM_HEAD_DIM
    row = lax.broadcasted_iota(jnp.int32, (L, L), 0)
    col = lax.broadcasted_iota(jnp.int32, (L, L), 1)
    ones = jnp.ones((L, LANES), BF16)

    def chain(bi, d, qv_ref, kt_ref, g_ref, gt_ref, h_ref):
        allowed = (col <= row) if d == 0 else (col >= row)
        gates = g_ref[bi] + bias_ref[...]
        gates_t = gt_ref[bi] + bias_t_ref[...]
        log2e = np.float32(np.log2(np.e))
        logf = _log_sigmoid(gates) * log2e
        logf_t = _log_sigmoid(gates_t) * log2e
        gates_t = gates_t * log2e
        sel_col = allowed.astype(BF16)
        sel_row = ((row <= col) if d == 0 else (row >= col)).astype(BF16)
        bcum = _dot3(logf, sel_col, x_is_lhs=False)
        bcum_t = _dot3(logf_t, sel_row)
        last = L - 1 if d == 0 else 0
        for hd in range(MLSTM_HEADS):
            idx = (bi * 2 + d) * MLSTM_HEADS + hd
            ig = 2 * MLSTM_HEADS * d + hd
            fg = ig + MLSTM_HEADS
            q = qv_ref[bi, :, hd * dh:(hd + 1) * dh]
            v = qv_ref[bi, :, MLSTM_WIDTH + hd * dh:MLSTM_WIDTH + (hd + 1) * dh]
            kt = kt_ref[bi, hd * dh:(hd + 1) * dh, :]
            v_ones = jnp.concatenate([v, ones], axis=1)
            a_row = gates_t[ig:ig + 1, :] - bcum_t[fg:fg + 1, :]
            b_last = jnp.broadcast_to(bcum_t[fg:fg + 1, last:last + 1], (1, L))
            b_col = jnp.broadcast_to(bcum[:, fg:fg + 1], (L, L))
            m_old = m_ref[idx]
            cn_old = cn_ref[idx]
            dlog = jnp.where(allowed, b_col + a_row, -jnp.inf)
            inter = b_col + m_old
            m_row = jnp.maximum(inter, jnp.max(dlog, axis=-1, keepdims=True))
            w = jnp.exp2(dlog - m_row) * _dot(q, kt)
            inter_w = jnp.exp2(inter[:, :dh] - m_row[:, :dh])
            intra = _dot(w.astype(BF16), v_ones)
            carry = _dot(q, cn_old.astype(BF16))
            num = intra[:, :dh] + inter_w * carry[:, :dh]
            den = intra[:, dh:] + inter_w * carry[:, dh:]
            hval = num / jnp.maximum(jnp.abs(den), jnp.exp2(-m_row[:, :dh]))
            h_ref[bi, :, hd * dh:(hd + 1) * dh] = hval.astype(BF16)
            m_new = jnp.maximum(b_last + m_old, jnp.max(b_last + a_row, axis=-1, keepdims=True))
            keep = jnp.exp2(b_last + m_old - m_new)
            wk_row = jnp.exp2(b_last + a_row - m_new)
            kw_t = (kt.astype(F32) * wk_row).astype(BF16)
            cn_ref[idx] = jnp.broadcast_to(keep[:, 0:1], (1, 2 * dh)) * cn_old + _dot(kw_t, v_ones)
            m_ref[idx] = m_new

    for bi in range(bb):
        chain(bi, 0, qvf_ref, ktf_ref, gf_ref, gtf_ref, hf_ref)
        chain(bi, 1, qvb_ref, ktb_ref, gb_ref, gtb_ref, hb_ref)


def _mlstm(qv, kt, gates, gt, bias, bias_t, batch, seq_len):
    chunk = min(MLSTM_CHUNK, seq_len)
    nc = seq_len // chunk
    bb = min(MLSTM_BATCH_BLOCK, batch)
    bias_t = jnp.broadcast_to(bias_t[:, 0:1], (MLSTM_GATES, chunk))
    qv = qv.reshape(batch, seq_len, qv.shape[-1])
    gates = gates.reshape(batch, seq_len, LANES)
    fwd = lambda b, c: (b, c, 0)
    bwd = lambda b, c: (b, nc - 1 - c, 0)
    fwd_t = lambda b, c: (b, 0, c)
    bwd_t = lambda b, c: (b, 0, nc - 1 - c)
    nstate = bb * 2 * MLSTM_HEADS
    hf, hb = pl.pallas_call(
        functools.partial(_mlstm_kernel, bb=bb, L=chunk),
        grid=(batch // bb, nc),
        in_specs=[
            pl.BlockSpec((bb, chunk, 2 * MLSTM_WIDTH), fwd),
            pl.BlockSpec((bb, MLSTM_WIDTH, chunk), fwd_t),
            pl.BlockSpec((bb, chunk, LANES), fwd),
            pl.BlockSpec((bb, MLSTM_GATES, chunk), fwd_t),
            pl.BlockSpec((bb, chunk, 2 * MLSTM_WIDTH), bwd),
            pl.BlockSpec((bb, MLSTM_WIDTH, chunk), bwd_t),
            pl.BlockSpec((bb, chunk, LANES), bwd),
            pl.BlockSpec((bb, MLSTM_GATES, chunk), bwd_t),
            pl.BlockSpec((1, LANES), lambda b, c: (0, 0)),
            pl.BlockSpec((MLSTM_GATES, chunk), lambda b, c: (0, 0)),
        ],
        out_specs=[
            pl.BlockSpec((bb, chunk, MLSTM_WIDTH), fwd),
            pl.BlockSpec((bb, chunk, MLSTM_WIDTH), bwd),
        ],
        out_shape=[jax.ShapeDtypeStruct((batch, seq_len, MLSTM_WIDTH), BF16)] * 2,
        scratch_shapes=[
            pltpu.VMEM((nstate, MLSTM_HEAD_DIM, 2 * MLSTM_HEAD_DIM), F32),
            pltpu.VMEM((nstate, 1, chunk), F32),
        ],
        compiler_params=_cparams(2),
        name="mlstm",
    )(qv, kt, gates, gt, qv, kt, gates, gt, bias, bias_t)
    return hf.reshape(batch * seq_len, MLSTM_WIDTH), hb.reshape(batch * seq_len, MLSTM_WIDTH)


def _ab_out_kernel(a_ref, hf_ref, hb_ref, mo_ref, gain_ref, w_ref, x_ref, o_ref):
    acc = x_ref[...] + _dot(a_ref[...], w_ref[0:ATTN_Q_WIDTH, :])
    for hd in range(MLSTM_HEADS):
        sl = slice(hd * MLSTM_HEAD_DIM, (hd + 1) * MLSTM_HEAD_DIM)
        y = _rms(hf_ref[:, sl].astype(F32) + hb_ref[:, sl].astype(F32), gain_ref[:, sl])
        z = (y * jax.nn.sigmoid(mo_ref[:, sl].astype(F32))).astype(BF16)
        acc = acc + _dot(z, w_ref[ATTN_Q_WIDTH + hd * MLSTM_HEAD_DIM:ATTN_Q_WIDTH + (hd + 1) * MLSTM_HEAD_DIM, :])
    o_ref[...] = acc


def _ab_out(attn, hf, hb, mo, gain, w, x, seq_len):
    t, d = x.shape
    tm = min(ROW_TILE, seq_len)
    row = lambda i: (i, 0)
    fixed = lambda i: (0, 0)
    return pl.pallas_call(
        _ab_out_kernel,
        grid=(t // tm,),
        in_specs=[
            pl.BlockSpec((tm, ATTN_Q_WIDTH), row),
            pl.BlockSpec((tm, MLSTM_WIDTH), row),
            pl.BlockSpec((tm, MLSTM_WIDTH), row),
            pl.BlockSpec((tm, MLSTM_WIDTH), row),
            pl.BlockSpec((1, MLSTM_WIDTH), fixed),
            pl.BlockSpec(w.shape, fixed),
            pl.BlockSpec((tm, d), row),
        ],
        out_specs=pl.BlockSpec((tm, d), row),
        out_shape=jax.ShapeDtypeStruct((t, d), F32),
        compiler_params=_cparams(1),
        name="ab_out",
    )(attn, hf, hb, mo, gain, w, x)


def _rope_angles(seq_len, head_dim):
    rows = seq_len // GRID_W
    row_idx = jnp.repeat(jnp.arange(rows, dtype=F32), GRID_W)
    col_idx = jnp.tile(jnp.arange(GRID_W, dtype=F32), rows)
    axis_dim = head_dim // 2
    inv_freq = ROPE_THETA ** (-jnp.arange(0, axis_dim, 2, dtype=F32) / axis_dim)
    ang = jnp.concatenate([row_idx[:, None] * inv_freq, col_idx[:, None] * inv_freq], axis=-1)
    return jnp.cos(ang), jnp.sin(ang)


def _prepare(p):
    d = p["ab_w_in"].shape[1]
    n_ab = p["ab_w_in"].shape[0]
    a_cols = ATTN_Q_WIDTH + 2 * ATTN_KV_WIDTH
    m_lo = a_cols
    m_hi = a_cols + 4 * MLSTM_WIDTH
    w_in = p["ab_w_in"]
    pad = jnp.zeros((n_ab, d, LANES - MLSTM_GATES), w_in.dtype)
    out = dict(
        ab_w_attn=jnp.concatenate([w_in[:, :, :a_cols], w_in[:, :, m_hi:], pad], axis=-1).astype(BF16),
        ab_w_mlstm=w_in[:, :, m_lo:m_hi].astype(BF16),
        ab_bias=jnp.pad(p["ab_gate_bias"].astype(F32), ((0, 0), (0, LANES - MLSTM_GATES)))[:, None, :],
        ab_bias_t=jnp.broadcast_to(p["ab_gate_bias"].astype(F32)[:, :, None], (n_ab, MLSTM_GATES, LANES)),
        ab_w_out=p["ab_w_out"].astype(BF16),
        attn_qg=jnp.tile(p["attn_q_norm"].astype(F32), (1, LANES // ATTN_HEAD_DIM))[:, None, :],
        attn_kg=jnp.tile(p["attn_k_norm"].astype(F32), (1, LANES // ATTN_HEAD_DIM))[:, None, :],
        mlstm_gain=p["mlstm_out_norm"].astype(F32)[:, None, :],
        ret_w_in=p["ret_w_in"].astype(BF16),
        ret_w_out=p["ret_w_out"].astype(BF16),
        ret_gain=p["ret_out_norm"].astype(F32)[:, None, :],
        ret_decay_logit=p["ret_decay_logit"],
        norm_mix=p["norm_mix"].astype(F32)[:, None, :],
        norm_ffn=p["norm_ffn"].astype(F32)[:, None, :],
        norm_final=p["norm_final"].astype(F32)[None, :],
    )
    depth, _, ff2 = p["ffn_w_up"].shape
    dff = ff2 // 2
    out["ffn_wu"] = p["ffn_w_up"][:, :, :dff].astype(BF16)
    out["ffn_wg"] = p["ffn_w_up"][:, :, dff:].astype(BF16)
    out["ffn_wd"] = p["ffn_w_down"].astype(BF16)
    out["ffn_cw"] = jnp.pad(p["ffn_conv_w"].astype(F32), ((0, 0), (0, 5), (0, 0)))
    out["ffn_cb"] = p["ffn_conv_b"].astype(F32)[:, None, :]
    return out


def _trunk(x3, w):
    batch, seq_len, d = x3.shape
    x = x3.reshape(batch * seq_len, d)
    cos_a, sin_a = _rope_angles(seq_len, ATTN_HEAD_DIM)
    reps = LANES // (ATTN_HEAD_DIM // 2)
    cos_a = jnp.tile(cos_a, (1, reps))
    sin_a = jnp.tile(jnp.concatenate([-sin_a, sin_a], axis=-1), (1, reps // 2))
    cos_r, sin_r = _rope_angles(seq_len, RET_QK_DIM)
    depth = w["norm_mix"].shape[0]
    for layer in range(depth):
        j = layer // 2
        if layer % 2 == 0:
            q, kt, v, gates, gt, qv, mkt, mo = _ab_proj(
                x, batch, seq_len, w["norm_mix"][layer], w["ab_w_attn"][j], w["ab_w_mlstm"][j],
                cos_a, sin_a, w["attn_qg"][j], w["attn_kg"][j])
            attn = _attention(q, kt, v, batch, seq_len).reshape(batch * seq_len, ATTN_Q_WIDTH)
            hf, hb = _mlstm(qv, mkt, gates, gt, w["ab_bias"][j], w["ab_bias_t"][j], batch, seq_len)
            x = _ab_out(attn, hf, hb, mo, w["mlstm_gain"][j], w["ab_w_out"][j], x, seq_len)
        else:
            rq, rkt, v, g = _ret_proj(x, batch, seq_len, w["norm_mix"][layer], w["ret_w_in"][j], cos_r, sin_r)
            tables = _ret_tables(w["ret_decay_logit"][j], min(RET_CHUNK, seq_len))
            yf, yb = _ret_core(rq, rkt, v, batch, seq_len, *tables)
            x = _ret_out(yf, yb, g, w["ret_gain"][j], w["ret_w_out"][j], x, seq_len)
        x = _ffn(x, seq_len, w["norm_ffn"][layer], w["ffn_wu"][layer], w["ffn_wg"][layer], w["ffn_cw"][layer],
                 w["ffn_cb"][layer], w["ffn_wd"][layer], w["norm_final"], final=(layer == depth - 1))
    return x.reshape(batch, seq_len, d)


def kernel(x_prompt, x_sample, norm_mix, norm_ffn, norm_final, ab_w_in, ab_gate_bias, attn_q_norm, attn_k_norm,
           mlstm_out_norm, ab_w_out, ret_w_in, ret_decay_logit, ret_out_norm, ret_w_out, ffn_w_up, ffn_conv_w,
           ffn_conv_b, ffn_w_down):
    w = _prepare(dict(
        norm_mix=norm_mix, norm_ffn=norm_ffn, norm_final=norm_final, ab_w_in=ab_w_in, ab_gate_bias=ab_gate_bias,
        attn_q_norm=attn_q_norm, attn_k_norm=attn_k_norm, mlstm_out_norm=mlstm_out_norm, ab_w_out=ab_w_out,
        ret_w_in=ret_w_in, ret_decay_logit=ret_decay_logit, ret_out_norm=ret_out_norm, ret_w_out=ret_w_out,
        ffn_w_up=ffn_w_up, ffn_conv_w=ffn_conv_w, ffn_conv_b=ffn_conv_b, ffn_w_down=ffn_w_down))
    return (_trunk(x_prompt, w), _trunk(x_sample, w))
```

```python
import functools

import numpy as np
import jax
import jax.numpy as jnp
from jax import lax
from jax.experimental import pallas as pl
from jax.experimental.pallas import tpu as pltpu

F32 = jnp.float32
BF16 = jnp.bfloat16

GRID_W = 64
ROPE_THETA = 10000.0
NORM_EPS = 1e-6
ATTN_HEADS = 8
ATTN_KV_HEADS = 2
ATTN_HEAD_DIM = 64
ATTN_Q_WIDTH = ATTN_HEADS * ATTN_HEAD_DIM
ATTN_KV_WIDTH = ATTN_KV_HEADS * ATTN_HEAD_DIM
MLSTM_HEADS = 4
MLSTM_HEAD_DIM = 128
MLSTM_WIDTH = MLSTM_HEADS * MLSTM_HEAD_DIM
MLSTM_GATES = 4 * MLSTM_HEADS
RET_HEADS = 4
RET_QK_DIM = 256
RET_V_DIM = 512
RET_QK_WIDTH = RET_HEADS * RET_QK_DIM
RET_V_WIDTH = RET_HEADS * RET_V_DIM
CHUNK = 128
MLSTM_CHUNK = 256
RET_CHUNK = 256

LANES = 128
BF16_SUBLANES = 16
VMEM_LIMIT_BYTES = 56 * 1024 * 1024

ROW_TILE = 512
FFN_ROW_TILE = 512
FFN_COL_TILE = 256
FFN_G_SLABS = 3
ATTN_Q_TILE = 256
ATTN_K_TILE = 2048
ATTN_SCORE_BYTES = 32 * 1024 * 1024
MLSTM_BATCH_BLOCK = 4


def _cparams(n_axes):
    return pltpu.CompilerParams(
        dimension_semantics=("arbitrary",) * n_axes, vmem_limit_bytes=VMEM_LIMIT_BYTES)


def _rms(x, gain):
    ms = jnp.mean(x * x, axis=-1, keepdims=True)
    return x * lax.rsqrt(ms + NORM_EPS) * gain


def _dot(a, b):
    return jnp.dot(a, b, preferred_element_type=F32)


def _dot_nt(a, b):
    return lax.dot_general(a, b, (((1,), (1,)), ((), ())), preferred_element_type=F32)


def _dot_tn(a, b):
    return lax.dot_general(a, b, (((0,), (0,)), ((), ())), preferred_element_type=F32)


def _split3(x):
    x1 = x.astype(BF16)
    r1 = x - x1.astype(F32)
    x2 = r1.astype(BF16)
    x3 = (r1 - x2.astype(F32)).astype(BF16)
    return x1, x2, x3


def _log_sigmoid(x):
    return jnp.minimum(x, 0.0) - jnp.log1p(jnp.exp(-jnp.abs(x)))


RET_MIX_SLICES = tuple((hd * RET_V_DIM, (hd + 1) * RET_V_DIM) for hd in range(RET_HEADS))
AB_MIX_SLICES = ((0, ATTN_Q_WIDTH), (ATTN_Q_WIDTH, ATTN_Q_WIDTH + MLSTM_WIDTH))


def _mix_retention(read, gain_ref, k):
    sl = slice(*RET_MIX_SLICES[k])
    y = _rms(read(0, sl).astype(F32) + read(1, sl).astype(F32), gain_ref[:, sl])
    gate = read(2, sl).astype(F32)
    return (y * (gate * jax.nn.sigmoid(gate))).astype(BF16)


def _mix_attn_mlstm(read, gain_ref, k):
    if k == 0:
        return read(0, slice(None))
    outs = []
    for hd in range(MLSTM_HEADS):
        sl = slice(hd * MLSTM_HEAD_DIM, (hd + 1) * MLSTM_HEAD_DIM)
        y = _rms(read(1, sl).astype(F32) + read(2, sl).astype(F32), gain_ref[:, sl])
        outs.append((y * jax.nn.sigmoid(read(3, sl).astype(F32))).astype(BF16))
    return jnp.concatenate(outs, axis=1)


def _ffn_kernel(*refs, tm, nf, tf, blocks_per_seq, final, mix_fn, mix_slices, n_mix):
    it = iter(refs)
    xp_ref, x_ref, xn_ref = next(it), next(it), next(it)
    mix_refs = [(next(it), next(it), next(it)) for _ in range(n_mix)]
    mgain_ref, wout_ref, gain_ref, wu_ref, wg_ref, cw_ref, cb_ref, wd_ref, fin_ref = (next(it) for _ in range(9))
    o_ref, h_ref, g_ref, a_ref, z_ref, xe_ref = (next(it) for _ in range(6))
    i = pl.program_id(0)
    halo = BF16_SUBLANES
    pos = i % blocks_per_seq
    gain = gain_ref[...]
    for part, lo, hi, xr in ((0, 0, halo, xp_ref), (1, halo, halo + tm, x_ref), (2, halo + tm, tm + 2 * halo, xn_ref)):
        read = lambda k, cols, part=part: mix_refs[k][part][:, cols]
        for k, (clo, chi) in enumerate(mix_slices):
            z_ref[lo:hi, clo:chi] = mix_fn(read, mgain_ref, k)
        xe_ref[lo:hi, :] = xr[...]
    xe_ref[...] += _dot(z_ref[...], wout_ref[...])
    hp = jnp.where(pos == 0, 0.0, _rms(xe_ref[0:halo, :], gain))
    hn = jnp.where(pos == blocks_per_seq - 1, 0.0, _rms(xe_ref[halo + tm:, :], gain))
    x = xe_ref[halo:halo + tm, :]
    h_ref[0:halo, :] = hp.astype(BF16)
    h_ref[halo:halo + tm, :] = _rms(x, gain).astype(BF16)
    h_ref[halo + tm:, :] = hn.astype(BF16)
    sqrt_half = np.float32(np.sqrt(0.5))
    for j in range(nf):
        s = j % FFN_G_SLABS
        cols = slice(j * tf, (j + 1) * tf)
        g_ref[s] = _dot(h_ref[...], wg_ref[:, cols])
        cw = cw_ref[:, cols]
        gc = (g_ref[s, halo - 1:halo - 1 + tm, :] * cw[0:1, :] + g_ref[s, halo:halo + tm, :] * cw[1:2, :]
              + g_ref[s, halo + 1:halo + 1 + tm, :] * cw[2:3, :] + cb_ref[:, cols])
        u = _dot(h_ref[halo:halo + tm, :], wu_ref[:, cols])
        act = 0.5 * gc * (1.0 + lax.erf(gc * sqrt_half))
        a_ref[:, cols] = (act * u).astype(BF16)
    y = x + _dot(a_ref[...], wd_ref[...])
    o_ref[...] = _rms(y, fin_ref[...]) if final else y


def _mixer_out_ffn(x, mix_inputs, mix_fn, mix_slices, mgain, wout, seq_len, gain, wu, wg, cw, cb, wd, fin_gain,
                   final):
    t, d = x.shape
    tm = min(FFN_ROW_TILE, seq_len)
    tf = FFN_COL_TILE
    nf = wu.shape[1] // tf
    halo = BF16_SUBLANES
    nblk = t // tm
    hb = tm // halo
    fixed = lambda i: (0, 0)

    def row_blocks(width):
        return [pl.BlockSpec((halo, width), lambda i: (jnp.maximum(i * hb - 1, 0), 0)),
                pl.BlockSpec((tm, width), lambda i: (i, 0)),
                pl.BlockSpec((halo, width), lambda i: (jnp.minimum((i + 1) * hb, nblk * hb - 1), 0))]

    mix_specs, mix_args = [], []
    for a in mix_inputs:
        mix_specs += row_blocks(a.shape[1])
        mix_args += [a, a, a]
    kern = functools.partial(_ffn_kernel, tm=tm, nf=nf, tf=tf, blocks_per_seq=seq_len // tm, final=final,
                             mix_fn=mix_fn, mix_slices=mix_slices, n_mix=len(mix_inputs))
    return pl.pallas_call(
        kern,
        grid=(nblk,),
        in_specs=row_blocks(d) + mix_specs + [
            pl.BlockSpec(mgain.shape, fixed),
            pl.BlockSpec(wout.shape, fixed, pipeline_mode=pl.Buffered(1)),
            pl.BlockSpec((1, d), lambda i: (0, 0)),
            pl.BlockSpec(wu.shape, fixed, pipeline_mode=pl.Buffered(1)),
            pl.BlockSpec(wg.shape, fixed, pipeline_mode=pl.Buffered(1)),
            pl.BlockSpec(cw.shape, fixed),
            pl.BlockSpec(cb.shape, fixed),
            pl.BlockSpec(wd.shape, fixed, pipeline_mode=pl.Buffered(1)),
            pl.BlockSpec((1, d), lambda i: (0, 0)),
        ],
        out_specs=pl.BlockSpec((tm, d), lambda i: (i, 0)),
        out_shape=jax.ShapeDtypeStruct((t, d), F32),
        scratch_shapes=[
            pltpu.VMEM((tm + 2 * halo, d), BF16),
            pltpu.VMEM((FFN_G_SLABS, tm + 2 * halo, tf), F32),
            pltpu.VMEM((tm, nf * tf), BF16),
            pltpu.VMEM((tm + 2 * halo, wout.shape[0]), BF16),
            pltpu.VMEM((tm + 2 * halo, d), F32),
        ],
        compiler_params=_cparams(1),
        name="mixer_out_ffn",
    )(x, x, x, *mix_args, mgain, wout, gain, wu, wg, cw, cb, wd, fin_gain)


def _ret_proj_kernel(x_ref, gain_ref, w_ref, cos_ref, sin_ref, q_ref, kt_ref, v_ref, g_ref):
    h = _rms(x_ref[...], gain_ref[...]).astype(BF16)
    c = cos_ref[...]
    s = sin_ref[...]
    half = RET_QK_DIM // 2
    for which in (0, 1):
        for hd in range(RET_HEADS):
            lo = hd * RET_QK_DIM
            acc = _dot(h, w_ref[:, which * RET_QK_WIDTH + lo:which * RET_QK_WIDTH + lo + RET_QK_DIM])
            x1 = acc[:, :half]
            x2 = acc[:, half:]
            r1 = x1 * c - x2 * s
            r2 = x2 * c + x1 * s
            if which == 0:
                q_ref[:, lo:lo + half] = r1.astype(BF16)
                q_ref[:, lo + half:lo + 2 * half] = r2.astype(BF16)
            else:
                scale = np.float32(RET_QK_DIM ** -0.5)
                kt_ref[0, lo:lo + half, :] = (r1 * scale).T.astype(BF16)
                kt_ref[0, lo + half:lo + 2 * half, :] = (r2 * scale).T.astype(BF16)
    v0 = 2 * RET_QK_WIDTH
    g0 = v0 + RET_V_WIDTH
    for hd in range(RET_HEADS):
        sl = slice(hd * RET_V_DIM, (hd + 1) * RET_V_DIM)
        v_ref[:, sl] = _dot(h, w_ref[:, v0 + hd * RET_V_DIM:v0 + (hd + 1) * RET_V_DIM]).astype(BF16)
        g_ref[:, sl] = _dot(h, w_ref[:, g0 + hd * RET_V_DIM:g0 + (hd + 1) * RET_V_DIM]).astype(BF16)


def _ret_proj(x, batch, seq_len, gain, w, cos, sin):
    t, d = x.shape
    tm = min(ROW_TILE, seq_len)
    sblk = seq_len // tm
    row = lambda i: (i, 0)
    return pl.pallas_call(
        _ret_proj_kernel,
        grid=(t // tm,),
        in_specs=[
            pl.BlockSpec((tm, d), row),
            pl.BlockSpec((1, d), lambda i: (0, 0)),
            pl.BlockSpec(w.shape, lambda i: (0, 0), pipeline_mode=pl.Buffered(1)),
            pl.BlockSpec((tm, LANES), lambda i: (i % sblk, 0)),
            pl.BlockSpec((tm, LANES), lambda i: (i % sblk, 0)),
        ],
        out_specs=[
            pl.BlockSpec((tm, RET_QK_WIDTH), row),
            pl.BlockSpec((1, RET_QK_WIDTH, tm), lambda i: (i // sblk, 0, i % sblk)),
            pl.BlockSpec((tm, RET_V_WIDTH), row),
            pl.BlockSpec((tm, RET_V_WIDTH), row),
        ],
        out_shape=[
            jax.ShapeDtypeStruct((t, RET_QK_WIDTH), BF16),
            jax.ShapeDtypeStruct((batch, RET_QK_WIDTH, seq_len), BF16),
            jax.ShapeDtypeStruct((t, RET_V_WIDTH), BF16),
            jax.ShapeDtypeStruct((t, RET_V_WIDTH), BF16),
        ],
        compiler_params=_cparams(1),
        name="ret_proj",
    )(x, gain, w, cos, sin)


def _ret_kernel(qf_ref, ktf_ref, vf_ref, qb_ref, ktb_ref, vb_ref, intra_ref, qd_ref, kd_ref, cd_ref,
                yf_ref, yb_ref, s_ref):
    c = pl.program_id(1)

    @pl.when(c == 0)
    def _():
        s_ref[...] = jnp.zeros_like(s_ref)

    dirs = ((qf_ref, ktf_ref, vf_ref, yf_ref), (qb_ref, ktb_ref, vb_ref, yb_ref))
    for d, (q_ref, kt_ref, v_ref, y_ref) in enumerate(dirs):
        for hd in range(RET_HEADS):
            idx = d * RET_HEADS + hd
            q = q_ref[:, hd * RET_QK_DIM:(hd + 1) * RET_QK_DIM]
            kt = kt_ref[0, hd * RET_QK_DIM:(hd + 1) * RET_QK_DIM, :]
            v = v_ref[:, hd * RET_V_DIM:(hd + 1) * RET_V_DIM]
            state = s_ref[idx]
            scores = _dot(q, kt) * intra_ref[d, hd]
            inter = _dot(q, state.astype(BF16)) * qd_ref[d, hd][:, 0:1]
            y_ref[:, hd * RET_V_DIM:(hd + 1) * RET_V_DIM] = (_dot(scores.astype(BF16), v) + inter).astype(BF16)
            kdec_t = (kt.astype(F32) * kd_ref[d, hd][0:1, :]).astype(BF16)
            s_ref[idx] = cd_ref[d, hd][0:1, 0:1] * state + _dot(kdec_t, v)


def _ret_core(q, kt, v, batch, seq_len, intra, qd, kd, cd):
    t = q.shape[0]
    chunk = intra.shape[-1]
    nc = seq_len // chunk
    fwd = lambda b, c: (b * nc + c, 0)
    bwd = lambda b, c: (b * nc + nc - 1 - c, 0)
    fwd_t = lambda b, c: (b, 0, c)
    bwd_t = lambda b, c: (b, 0, nc - 1 - c)
    const4 = lambda b, c: (0, 0, 0, 0)
    return pl.pallas_call(
        _ret_kernel,
        grid=(batch, nc),
        in_specs=[
            pl.BlockSpec((chunk, RET_QK_WIDTH), fwd),
            pl.BlockSpec((1, RET_QK_WIDTH, chunk), fwd_t),
            pl.BlockSpec((chunk, RET_V_WIDTH), fwd),
            pl.BlockSpec((chunk, RET_QK_WIDTH), bwd),
            pl.BlockSpec((1, RET_QK_WIDTH, chunk), bwd_t),
            pl.BlockSpec((chunk, RET_V_WIDTH), bwd),
            pl.BlockSpec(intra.shape, const4),
            pl.BlockSpec(qd.shape, const4),
            pl.BlockSpec(kd.shape, const4),
            pl.BlockSpec(cd.shape, const4),
        ],
        out_specs=[
            pl.BlockSpec((chunk, RET_V_WIDTH), fwd),
            pl.BlockSpec((chunk, RET_V_WIDTH), bwd),
        ],
        out_shape=[jax.ShapeDtypeStruct((t, RET_V_WIDTH), BF16)] * 2,
        scratch_shapes=[pltpu.VMEM((2 * RET_HEADS, RET_QK_DIM, RET_V_DIM), F32)],
        compiler_params=_cparams(2),
        name="retention",
    )(q, kt, v, q, kt, v, intra, qd, kd, cd)


def _ret_out_kernel(yf_ref, yb_ref, g_ref, gain_ref, w_ref, x_ref, o_ref):
    acc = x_ref[...]
    for hd in range(RET_HEADS):
        sl = slice(hd * RET_V_DIM, (hd + 1) * RET_V_DIM)
        y = _rms(yf_ref[:, sl].astype(F32) + yb_ref[:, sl].astype(F32), gain_ref[:, sl])
        g = g_ref[:, sl].astype(F32)
        z = (y * (g * jax.nn.sigmoid(g))).astype(BF16)
        acc = acc + _dot(z, w_ref[sl, :])
    o_ref[...] = acc


def _ret_out(yf, yb, g, gain, w, x, seq_len):
    t, d = x.shape
    tm = min(ROW_TILE, seq_len)
    row = lambda i: (i, 0)
    fixed = lambda i: (0, 0)
    return pl.pallas_call(
        _ret_out_kernel,
        grid=(t // tm,),
        in_specs=[
            pl.BlockSpec((tm, RET_V_WIDTH), row),
            pl.BlockSpec((tm, RET_V_WIDTH), row),
            pl.BlockSpec((tm, RET_V_WIDTH), row),
            pl.BlockSpec((1, RET_V_WIDTH), fixed),
            pl.BlockSpec((RET_V_WIDTH, d), fixed),
            pl.BlockSpec((tm, d), row),
        ],
        out_specs=pl.BlockSpec((tm, d), row),
        out_shape=jax.ShapeDtypeStruct((t, d), F32),
        compiler_params=_cparams(1),
        name="ret_out",
    )(yf, yb, g, gain, w, x)


def _ret_tables(decay_logit, chunk):
    log_gamma = jax.nn.log_sigmoid(decay_logit.astype(F32))
    pos = jnp.arange(chunk, dtype=F32)
    diff = pos[:, None] - pos[None, :]
    lg = log_gamma[:, :, None, None]
    tri = jnp.exp(jnp.maximum(diff, 0.0)[None, None] * lg)
    intra_f = jnp.where((diff >= 0)[None], tri[0], 0.0)
    intra_b = jnp.where((diff > 0)[None], tri[1], 0.0)
    intra_b = intra_b[:, ::-1, ::-1]
    qdec = jnp.exp((pos + 1.0)[None, None, :] * log_gamma[:, :, None])
    kdec = jnp.exp((chunk - 1.0 - pos)[None, None, :] * log_gamma[:, :, None])
    qdec = jnp.stack([qdec[0], qdec[1, :, ::-1]])
    kdec = jnp.stack([kdec[0], kdec[1, :, ::-1]])
    cdec = jnp.exp(chunk * log_gamma)
    rep = lambda a: jnp.broadcast_to(a[..., None], a.shape + (LANES,))
    return (jnp.stack([intra_f, intra_b]), rep(qdec), jnp.broadcast_to(kdec[:, :, None, :], kdec.shape[:2] + (8, chunk)),
            jnp.broadcast_to(cdec[:, :, None, None], cdec.shape + (8, LANES)))


def _ab_proj_kernel(x_ref, gain_ref, w_ref, wm_ref, cos_ref, sin_ref, qg_ref, kg_ref,
                    q_ref, kt_ref, v_ref, g_ref, gt_ref, mqv_ref, mkt_ref, mo_ref, *, tm):
    h = _rms(x_ref[...], gain_ref[...]).astype(BF16)
    lane = lax.broadcasted_iota(jnp.int32, (tm, LANES), 1)
    ri = lax.broadcasted_iota(jnp.int32, (LANES, LANES), 0)
    ci = lax.broadcasted_iota(jnp.int32, (LANES, LANES), 1)
    same_head = ((ri // ATTN_HEAD_DIM) == (ci // ATTN_HEAD_DIM)).astype(BF16)

    def head_sumsq(x):
        x2 = x * x
        hi = x2.astype(BF16)
        lo = (x2 - hi.astype(F32)).astype(BF16)
        return _dot(hi, same_head) + _dot(lo, same_head)

    pa = _dot(h, w_ref[...])
    nqk = (ATTN_Q_WIDTH + ATTN_KV_WIDTH) // LANES
    sumsq = [head_sumsq(pa[:, p * LANES:(p + 1) * LANES]) for p in range(nqk)]
    n = MLSTM_WIDTH
    mqv_ref[:, 0:n] = _dot(h, wm_ref[:, 0:n]).astype(BF16)
    mkt_ref[0] = (_dot(h, wm_ref[:, n:2 * n]) * np.float32(MLSTM_HEAD_DIM ** -0.5)).T.astype(BF16)
    mqv_ref[:, n:2 * n] = _dot(h, wm_ref[:, 2 * n:3 * n]).astype(BF16)
    mo_ref[...] = _dot(h, wm_ref[:, 3 * n:4 * n]).astype(BF16)
    gates = pa[:, ATTN_Q_WIDTH + 2 * LANES:]
    g_ref[...] = gates
    gt_ref[0] = gates.T[0:MLSTM_GATES, :]
    c = cos_ref[...]
    s = sin_ref[...]
    first_half = (lane % ATTN_HEAD_DIM) < (ATTN_HEAD_DIM // 2)
    low = lane < ATTN_HEAD_DIM

    def norm_rope(p, gain):
        x = pa[:, p * LANES:(p + 1) * LANES]
        xn = x * lax.rsqrt(sumsq[p] * np.float32(1.0 / ATTN_HEAD_DIM) + NORM_EPS) * gain
        swapped = jnp.where(first_half, pltpu.roll(xn, LANES - ATTN_HEAD_DIM // 2, 1),
                            pltpu.roll(xn, ATTN_HEAD_DIM // 2, 1))
        return xn * c + swapped * s

    for p in range(ATTN_HEADS // 2):
        y = norm_rope(p, qg_ref[...]) * np.float32(ATTN_HEAD_DIM ** -0.5 * np.log2(np.e))
        yr = pltpu.roll(y, ATTN_HEAD_DIM, 1)
        if (2 * p) // (ATTN_HEADS // ATTN_KV_HEADS) == 0:
            head_a, head_b = jnp.where(low, y, 0.0), jnp.where(low, yr, 0.0)
        else:
            head_a, head_b = jnp.where(low, 0.0, yr), jnp.where(low, 0.0, y)
        q_ref[0, 2 * p] = head_a.astype(BF16)
        q_ref[0, 2 * p + 1] = head_b.astype(BF16)
    k = norm_rope(nqk - 1, kg_ref[...])
    kt_ref[0] = k.T.astype(BF16)
    v_ref[0] = pa[:, ATTN_Q_WIDTH + LANES:ATTN_Q_WIDTH + 2 * LANES].astype(BF16)


def _ab_proj(x, batch, seq_len, gain, w, wm, cos, sin, qg, kg):
    t, d = x.shape
    tm = min(ROW_TILE, seq_len)
    sblk = seq_len // tm
    rows = lambda b, s: (b * sblk + s, 0)
    return pl.pallas_call(
        functools.partial(_ab_proj_kernel, tm=tm),
        grid=(batch, sblk),
        in_specs=[
            pl.BlockSpec((tm, d), rows),
            pl.BlockSpec((1, d), lambda b, s: (0, 0)),
            pl.BlockSpec(w.shape, lambda b, s: (0, 0), pipeline_mode=pl.Buffered(1)),
            pl.BlockSpec(wm.shape, lambda b, s: (0, 0), pipeline_mode=pl.Buffered(1)),
            pl.BlockSpec((tm, LANES), lambda b, s: (s, 0)),
            pl.BlockSpec((tm, LANES), lambda b, s: (s, 0)),
            pl.BlockSpec((1, LANES), lambda b, s: (0, 0)),
            pl.BlockSpec((1, LANES), lambda b, s: (0, 0)),
        ],
        out_specs=[
            pl.BlockSpec((1, ATTN_HEADS, tm, LANES), lambda b, s: (b, 0, s, 0)),
            pl.BlockSpec((1, LANES, tm), lambda b, s: (b, 0, s)),
            pl.BlockSpec((1, tm, LANES), lambda b, s: (b, s, 0)),
            pl.BlockSpec((tm, LANES), rows),
            pl.BlockSpec((1, MLSTM_GATES, tm), lambda b, s: (b, 0, s)),
            pl.BlockSpec((tm, 2 * MLSTM_WIDTH), rows),
            pl.BlockSpec((1, MLSTM_WIDTH, tm), lambda b, s: (b, 0, s)),
            pl.BlockSpec((tm, MLSTM_WIDTH), rows),
        ],
        out_shape=[
            jax.ShapeDtypeStruct((batch, ATTN_HEADS, seq_len, LANES), BF16),
            jax.ShapeDtypeStruct((batch, LANES, seq_len), BF16),
            jax.ShapeDtypeStruct((batch, seq_len, LANES), BF16),
            jax.ShapeDtypeStruct((t, LANES), F32),
            jax.ShapeDtypeStruct((batch, MLSTM_GATES, seq_len), F32),
            jax.ShapeDtypeStruct((t, 2 * MLSTM_WIDTH), BF16),
            jax.ShapeDtypeStruct((batch, MLSTM_WIDTH, seq_len), BF16),
            jax.ShapeDtypeStruct((t, MLSTM_WIDTH), BF16),
        ],
        compiler_params=_cparams(2),
        name="ab_proj",
    )(x, gain, w, wm, cos, sin, qg, kg)


def _attn_kernel(q_ref, kt_ref, v_ref, o_ref, *, tq, bounds):
    rows = ATTN_HEADS * tq
    q = q_ref[0].reshape(rows, LANES)
    m = jnp.full((rows, 1), -jnp.inf, F32)
    l = jnp.zeros((rows, 1), F32)
    acc = jnp.zeros((rows, LANES), F32)
    for lo, hi in zip(bounds[:-1], bounds[1:]):
        s = _dot(q, kt_ref[0, :, lo:hi])
        m_new = jnp.maximum(m, jnp.max(s, axis=-1, keepdims=True))
        alpha = jnp.exp2(m - m_new)
        p = jnp.exp2(s - m_new)
        l = alpha * l + jnp.sum(p, axis=-1, keepdims=True)
        acc = alpha * acc + _dot(p.astype(BF16), v_ref[0, lo:hi, :])
        m = m_new
    o = acc / l
    lane = lax.broadcasted_iota(jnp.int32, (tq, LANES), 1)
    low = lane < ATTN_HEAD_DIM
    for p2 in range(ATTN_HEADS // 2):
        a = o[(2 * p2) * tq:(2 * p2 + 1) * tq]
        b = o[(2 * p2 + 1) * tq:(2 * p2 + 2) * tq]
        if (2 * p2) // (ATTN_HEADS // ATTN_KV_HEADS) == 0:
            pair = jnp.where(low, a, pltpu.roll(b, ATTN_HEAD_DIM, 1))
        else:
            pair = jnp.where(low, pltpu.roll(a, ATTN_HEAD_DIM, 1), b)
        o_ref[0, :, p2 * LANES:(p2 + 1) * LANES] = pair.astype(BF16)


def _attention(q, kt, v, batch, seq_len):
    tq = ATTN_Q_TILE
    while tq > CHUNK and (ATTN_HEADS * tq * seq_len * 4 > ATTN_SCORE_BYTES or seq_len % tq):
        tq //= 2
    tk = min(ATTN_K_TILE, max(seq_len // 2, CHUNK))
    return pl.pallas_call(
        functools.partial(_attn_kernel, tq=tq, bounds=tuple(range(0, seq_len + 1, tk))),
        grid=(batch, seq_len // tq),
        in_specs=[
            pl.BlockSpec((1, ATTN_HEADS, tq, LANES), lambda b, i: (b, 0, i, 0)),
            pl.BlockSpec((1, LANES, seq_len), lambda b, i: (b, 0, 0)),
            pl.BlockSpec((1, seq_len, LANES), lambda b, i: (b, 0, 0)),
        ],
        out_specs=pl.BlockSpec((1, tq, ATTN_Q_WIDTH), lambda b, i: (b, i, 0)),
        out_shape=jax.ShapeDtypeStruct((batch, seq_len, ATTN_Q_WIDTH), BF16),
        compiler_params=_cparams(2),
        name="attention",
    )(q, kt, v)


def _dot3(x, m01, x_is_lhs=True):
    parts = _split3(x)
    if x_is_lhs:
        return _dot(parts[0], m01) + _dot(parts[1], m01) + _dot(parts[2], m01)
    return _dot(m01, parts[0]) + _dot(m01, parts[1]) + _dot(m01, parts[2])


def _mlstm_kernel(qvf_ref, ktf_ref, gf_ref, gtf_ref, qvb_ref, ktb_ref, gb_ref, gtb_ref, bias_ref, bias_t_ref,
                  hf_ref, hb_ref, cn_ref, m_ref, *, bb, L):
    c = pl.program_id(1)

    @pl.when(c == 0)
    def _():
        cn_ref[...] = jnp.zeros_like(cn_ref)
        m_ref[...] = jnp.zeros_like(m_ref)

    dh = MLSTM_HEAD_DIM
    row = lax.broadcasted_iota(jnp.int32, (L, L), 0)
    col = lax.broadcasted_iota(jnp.int32, (L, L), 1)
    ones = jnp.ones((L, LANES), BF16)

    def chain(bi, d, qv_ref, kt_ref, g_ref, gt_ref, h_ref):
        allowed = (col <= row) if d == 0 else (col >= row)
        gates = g_ref[bi] + bias_ref[...]
        gates_t = gt_ref[bi] + bias_t_ref[...]
        log2e = np.float32(np.log2(np.e))
        logf = _log_sigmoid(gates) * log2e
        logf_t = _log_sigmoid(gates_t) * log2e
        gates_t = gates_t * log2e
        sel_col = allowed.astype(BF16)
        sel_row = ((row <= col) if d == 0 else (row >= col)).astype(BF16)
        bcum = _dot3(logf, sel_col, x_is_lhs=False)
        bcum_t = _dot3(logf_t, sel_row)
        last = L - 1 if d == 0 else 0
        for hd in range(MLSTM_HEADS):
            idx = (bi * 2 + d) * MLSTM_HEADS + hd
            ig = 2 * MLSTM_HEADS * d + hd
            fg = ig + MLSTM_HEADS
            q = qv_ref[bi, :, hd * dh:(hd + 1) * dh]
            v = qv_ref[bi, :, MLSTM_WIDTH + hd * dh:MLSTM_WIDTH + (hd + 1) * dh]
            kt = kt_ref[bi, hd * dh:(hd + 1) * dh, :]
            v_ones = jnp.concatenate([v, ones], axis=1)
            a_row = gates_t[ig:ig + 1, :] - bcum_t[fg:fg + 1, :]
            b_last = jnp.broadcast_to(bcum_t[fg:fg + 1, last:last + 1], (1, L))
            b_col = jnp.broadcast_to(bcum[:, fg:fg + 1], (L, L))
            m_old = m_ref[idx]
            cn_old = cn_ref[idx]
            dlog = jnp.where(allowed, b_col + a_row, -jnp.inf)
            inter = b_col + m_old
            m_row = jnp.maximum(inter, jnp.max(dlog, axis=-1, keepdims=True))
            w = jnp.exp2(dlog - m_row) * _dot(q, kt)
            inter_w = jnp.exp2(inter[:, :dh] - m_row[:, :dh])
            intra = _dot(w.astype(BF16), v_ones)
            carry = _dot(q, cn_old.astype(BF16))
            num = intra[:, :dh] + inter_w * carry[:, :dh]
            den = intra[:, dh:] + inter_w * carry[:, dh:]
            hval = num / jnp.maximum(jnp.abs(den), jnp.exp2(-m_row[:, :dh]))
            h_ref[bi, :, hd * dh:(hd + 1) * dh] = hval.astype(BF16)
            m_new = jnp.maximum(b_last + m_old, jnp.max(b_last + a_row, axis=-1, keepdims=True))
            keep = jnp.exp2(b_last + m_old - m_new)
            wk_row = jnp.exp2(b_last + a_row - m_new)
            kw_t = (kt.astype(F32) * wk_row).astype(BF16)
            cn_ref[idx] = jnp.broadcast_to(keep[:, 0:1], (1, 2 * dh)) * cn_old + _dot(kw_t, v_ones)
            m_ref[idx] = m_new

    for bi in range(bb):
        chain(bi, 0, qvf_ref, ktf_ref, gf_ref, gtf_ref, hf_ref)
        chain(bi, 1, qvb_ref, ktb_ref, gb_ref, gtb_ref, hb_ref)


def _mlstm(qv, kt, gates, gt, bias, bias_t, batch, seq_len):
    chunk = min(MLSTM_CHUNK, seq_len)
    nc = seq_len // chunk
    bb = min(MLSTM_BATCH_BLOCK, batch)
    bias_t = jnp.broadcast_to(bias_t[:, 0:1], (MLSTM_GATES, chunk))
    qv = qv.reshape(batch, seq_len, qv.shape[-1])
    gates = gates.reshape(batch, seq_len, LANES)
    fwd = lambda b, c: (b, c, 0)
    bwd = lambda b, c: (b, nc - 1 - c, 0)
    fwd_t = lambda b, c: (b, 0, c)
    bwd_t = lambda b, c: (b, 0, nc - 1 - c)
    nstate = bb * 2 * MLSTM_HEADS
    hf, hb = pl.pallas_call(
        functools.partial(_mlstm_kernel, bb=bb, L=chunk),
        grid=(batch // bb, nc),
        in_specs=[
            pl.BlockSpec((bb, chunk, 2 * MLSTM_WIDTH), fwd),
            pl.BlockSpec((bb, MLSTM_WIDTH, chunk), fwd_t),
            pl.BlockSpec((bb, chunk, LANES), fwd),
            pl.BlockSpec((bb, MLSTM_GATES, chunk), fwd_t),
            pl.BlockSpec((bb, chunk, 2 * MLSTM_WIDTH), bwd),
            pl.BlockSpec((bb, MLSTM_WIDTH, chunk), bwd_t),
            pl.BlockSpec((bb, chunk, LANES), bwd),
            pl.BlockSpec((bb, MLSTM_GATES, chunk), bwd_t),
            pl.BlockSpec((1, LANES), lambda b, c: (0, 0)),
            pl.BlockSpec((MLSTM_GATES, chunk), lambda b, c: (0, 0)),
        ],
        out_specs=[
            pl.BlockSpec((bb, chunk, MLSTM_WIDTH), fwd),
            pl.BlockSpec((bb, chunk, MLSTM_WIDTH), bwd),
        ],
        out_shape=[jax.ShapeDtypeStruct((batch, seq_len, MLSTM_WIDTH), BF16)] * 2,
        scratch_shapes=[
            pltpu.VMEM((nstate, MLSTM_HEAD_DIM, 2 * MLSTM_HEAD_DIM), F32),
            pltpu.VMEM((nstate, 1, chunk), F32),
        ],
        compiler_params=_cparams(2),
        name="mlstm",
    )(qv, kt, gates, gt, qv, kt, gates, gt, bias, bias_t)
    return hf.reshape(batch * seq_len, MLSTM_WIDTH), hb.reshape(batch * seq_len, MLSTM_WIDTH)


def _ab_out_kernel(a_ref, hf_ref, hb_ref, mo_ref, gain_ref, w_ref, x_ref, o_ref):
    acc = x_ref[...] + _dot(a_ref[...], w_ref[0:ATTN_Q_WIDTH, :])
    for hd in range(MLSTM_HEADS):
        sl = slice(hd * MLSTM_HEAD_DIM, (hd + 1) * MLSTM_HEAD_DIM)
        y = _rms(hf_ref[:, sl].astype(F32) + hb_ref[:, sl].astype(F32), gain_ref[:, sl])
        z = (y * jax.nn.sigmoid(mo_ref[:, sl].astype(F32))).astype(BF16)
        acc = acc + _dot(z, w_ref[ATTN_Q_WIDTH + hd * MLSTM_HEAD_DIM:ATTN_Q_WIDTH + (hd + 1) * MLSTM_HEAD_DIM, :])
    o_ref[...] = acc


def _ab_out(attn, hf, hb, mo, gain, w, x, seq_len):
    t, d = x.shape
    tm = min(ROW_TILE, seq_len)
    row = lambda i: (i, 0)
    fixed = lambda i: (0, 0)
    return pl.pallas_call(
        _ab_out_kernel,
        grid=(t // tm,),
        in_specs=[
            pl.BlockSpec((tm, ATTN_Q_WIDTH), row),
            pl.BlockSpec((tm, MLSTM_WIDTH), row),
            pl.BlockSpec((tm, MLSTM_WIDTH), row),
            pl.BlockSpec((tm, MLSTM_WIDTH), row),
            pl.BlockSpec((1, MLSTM_WIDTH), fixed),
            pl.BlockSpec(w.shape, fixed),
            pl.BlockSpec((tm, d), row),
        ],
        out_specs=pl.BlockSpec((tm, d), row),
        out_shape=jax.ShapeDtypeStruct((t, d), F32),
        compiler_params=_cparams(1),
        name="ab_out",
    )(attn, hf, hb, mo, gain, w, x)


def _rope_angles(seq_len, head_dim):
    rows = seq_len // GRID_W
    row_idx = jnp.repeat(jnp.arange(rows, dtype=F32), GRID_W)
    col_idx = jnp.tile(jnp.arange(GRID_W, dtype=F32), rows)
    axis_dim = head_dim // 2
    inv_freq = ROPE_THETA ** (-jnp.arange(0, axis_dim, 2, dtype=F32) / axis_dim)
    ang = jnp.concatenate([row_idx[:, None] * inv_freq, col_idx[:, None] * inv_freq], axis=-1)
    return jnp.cos(ang), jnp.sin(ang)


def _prepare(p):
    d = p["ab_w_in"].shape[1]
    n_ab = p["ab_w_in"].shape[0]
    a_cols = ATTN_Q_WIDTH + 2 * ATTN_KV_WIDTH
    m_lo = a_cols
    m_hi = a_cols + 4 * MLSTM_WIDTH
    w_in = p["ab_w_in"]
    pad = jnp.zeros((n_ab, d, LANES - MLSTM_GATES), w_in.dtype)
    out = dict(
        ab_w_attn=jnp.concatenate([w_in[:, :, :a_cols], w_in[:, :, m_hi:], pad], axis=-1).astype(BF16),
        ab_w_mlstm=w_in[:, :, m_lo:m_hi].astype(BF16),
        ab_bias=jnp.pad(p["ab_gate_bias"].astype(F32), ((0, 0), (0, LANES - MLSTM_GATES)))[:, None, :],
        ab_bias_t=jnp.broadcast_to(p["ab_gate_bias"].astype(F32)[:, :, None], (n_ab, MLSTM_GATES, LANES)),
        ab_w_out=p["ab_w_out"].astype(BF16),
        attn_qg=jnp.tile(p["attn_q_norm"].astype(F32), (1, LANES // ATTN_HEAD_DIM))[:, None, :],
        attn_kg=jnp.tile(p["attn_k_norm"].astype(F32), (1, LANES // ATTN_HEAD_DIM))[:, None, :],
        mlstm_gain=p["mlstm_out_norm"].astype(F32)[:, None, :],
        ret_w_in=p["ret_w_in"].astype(BF16),
        ret_w_out=p["ret_w_out"].astype(BF16),
        ret_gain=p["ret_out_norm"].astype(F32)[:, None, :],
        ret_decay_logit=p["ret_decay_logit"],
        norm_mix=p["norm_mix"].astype(F32)[:, None, :],
        norm_ffn=p["norm_ffn"].astype(F32)[:, None, :],
        norm_final=p["norm_final"].astype(F32)[None, :],
    )
    depth, _, ff2 = p["ffn_w_up"].shape
    dff = ff2 // 2
    out["ffn_wu"] = p["ffn_w_up"][:, :, :dff].astype(BF16)
    out["ffn_wg"] = p["ffn_w_up"][:, :, dff:].astype(BF16)
    out["ffn_wd"] = p["ffn_w_down"].astype(BF16)
    out["ffn_cw"] = jnp.pad(p["ffn_conv_w"].astype(F32), ((0, 0), (0, 5), (0, 0)))
    out["ffn_cb"] = p["ffn_conv_b"].astype(F32)[:, None, :]
    return out


def _trunk(x3, w):
    batch, seq_len, d = x3.shape
    x = x3.reshape(batch * seq_len, d)
    cos_a, sin_a = _rope_angles(seq_len, ATTN_HEAD_DIM)
    reps = LANES // (ATTN_HEAD_DIM // 2)
    cos_a = jnp.tile(cos_a, (1, reps))
    sin_a = jnp.tile(jnp.concatenate([-sin_a, sin_a], axis=-1), (1, reps // 2))
    cos_r, sin_r = _rope_angles(seq_len, RET_QK_DIM)
    depth = w["norm_mix"].shape[0]
    for layer in range(depth):
        j = layer // 2
        if layer % 2 == 0:
            q, kt, v, gates, gt, qv, mkt, mo = _ab_proj(
                x, batch, seq_len, w["norm_mix"][layer], w["ab_w_attn"][j], w["ab_w_mlstm"][j],
                cos_a, sin_a, w["attn_qg"][j], w["attn_kg"][j])
            attn = _attention(q, kt, v, batch, seq_len).reshape(batch * seq_len, ATTN_Q_WIDTH)
            hf, hb = _mlstm(qv, mkt, gates, gt, w["ab_bias"][j], w["ab_bias_t"][j], batch, seq_len)
            mix = ((attn, hf, hb, mo), _mix_attn_mlstm, AB_MIX_SLICES, w["mlstm_gain"][j], w["ab_w_out"][j])
        else:
            rq, rkt, v, g = _ret_proj(x, batch, seq_len, w["norm_mix"][layer], w["ret_w_in"][j], cos_r, sin_r)
            tables = _ret_tables(w["ret_decay_logit"][j], min(RET_CHUNK, seq_len))
            yf, yb = _ret_core(rq, rkt, v, batch, seq_len, *tables)
            mix = ((yf, yb, g), _mix_retention, RET_MIX_SLICES, w["ret_gain"][j], w["ret_w_out"][j])
        x = _mixer_out_ffn(x, *mix, seq_len, w["norm_ffn"][layer], w["ffn_wu"][layer], w["ffn_wg"][layer],
                           w["ffn_cw"][layer], w["ffn_cb"][layer], w["ffn_wd"][layer], w["norm_final"],
                           final=(layer == depth - 1))
    return x.reshape(batch, seq_len, d)


def kernel(x_prompt, x_sample, norm_mix, norm_ffn, norm_final, ab_w_in, ab_gate_bias, attn_q_norm, attn_k_norm,
           mlstm_out_norm, ab_w_out, ret_w_in, ret_decay_logit, ret_out_norm, ret_w_out, ffn_w_up, ffn_conv_w,
           ffn_conv_b, ffn_w_down):
    w = _prepare(dict(
        norm_mix=norm_mix, norm_ffn=norm_ffn, norm_final=norm_final, ab_w_in=ab_w_in, ab_gate_bias=ab_gate_bias,
        attn_q_norm=attn_q_norm, attn_k_norm=attn_k_norm, mlstm_out_norm=mlstm_out_norm, ab_w_out=ab_w_out,
        ret_w_in=ret_w_in, ret_decay_logit=ret_decay_logit, ret_out_norm=ret_out_norm, ret_w_out=ret_w_out,
        ffn_w_up=ffn_w_up, ffn_conv_w=ffn_conv_w, ffn_conv_b=ffn_conv_b, ffn_w_down=ffn_w_down))
    return (_trunk(x_prompt, w), _trunk(x_sample, w))
```
